```python
import jax, jax.numpy as jnp
from jax import lax
import numpy as np

D_MODEL = 1024
BATCH = 16
SEQ = 4096
DEPTH = 4

N_MIXERS = 4
HEAD_DIM = 64
FOX_HEADS = D_MODEL // HEAD_DIM
SB_HEADS = D_MODEL // HEAD_DIM
Q_BLOCK = 128
GM_CHUNK = 128
GM_WIDTH = D_MODEL
GM_GROUPS = 8
GM_GROUP_DIM = GM_WIDTH // GM_GROUPS
CONV_WIDTH = 31
FFN_HIDDEN = -(-8 * D_MODEL // (3 * 256)) * 256
DN_ALPHA = (2.0 * DEPTH) ** 0.25
DN_BETA = (8.0 * DEPTH) ** -0.25
N_GM = (DEPTH + 3) // N_MIXERS
N_FOX = (DEPTH + 2) // N_MIXERS
N_SB = (DEPTH + 1) // N_MIXERS
N_CV = DEPTH // N_MIXERS
LN_EPS = 1e-5
NEG_INF = -1e30

kernel_name = "hybrid_interleaved_gmlp_fox_stickbreak_conformer"


def _layer_norm(x, g, b):
    xf = x.astype(jnp.float32)
    mu = jnp.mean(xf, axis=-1, keepdims=True)
    xc = xf - mu
    var = jnp.mean(xc * xc, axis=-1, keepdims=True)
    return (xc * lax.rsqrt(var + LN_EPS) * g + b).astype(x.dtype)


def _gmlp_mixer(h, w_in, b_in, ln_g, ln_b, w_s, b_s, w_out):
    B, S, _ = h.shape
    z = jax.nn.gelu(h @ w_in + b_in, approximate=False)
    u, v = jnp.split(z, 2, axis=-1)
    v = _layer_norm(v, ln_g, ln_b)
    nc = S // GM_CHUNK
    v5 = v.reshape(B, nc, GM_CHUNK, GM_GROUPS, GM_GROUP_DIM)
    causal = jnp.tril(jnp.ones((GM_CHUNK, GM_CHUNK), dtype=w_s.dtype))
    w_m = w_s * causal
    sv = jnp.einsum('gts,bnsgc->bntgc', w_m, v5) + b_s.T[None, None, :, :, None]
    y = u * sv.reshape(B, S, GM_WIDTH)
    return y @ w_out


def _fox_mixer(h, w_in, b_f, w_out):
    B, S, D = h.shape
    H = FOX_HEADS
    scale = HEAD_DIM ** -0.5
    proj = h @ w_in
    q, k, v, f_logit = jnp.split(proj, [D, 2 * D, 3 * D], axis=-1)
    to_heads = lambda t: t.reshape(B, S, H, HEAD_DIM).transpose(0, 2, 1, 3)
    q, k, v = to_heads(q), to_heads(k), to_heads(v)
    log_f = jax.nn.log_sigmoid((f_logit + b_f).astype(jnp.float32))
    F = jnp.cumsum(log_f, axis=1).transpose(0, 2, 1)
    nb = S // Q_BLOCK
    kpos = jnp.arange(S)
    qb = q.reshape(B, H, nb, Q_BLOCK, HEAD_DIM).transpose(2, 0, 1, 3, 4)
    Fb = F.reshape(B, H, nb, Q_BLOCK).transpose(2, 0, 1, 3)
    pb = kpos.reshape(nb, Q_BLOCK)

    def block(args):
        q_blk, F_blk, q_pos = args
        s = jnp.einsum('bhqd,bhkd->bhqk', q_blk, k).astype(jnp.float32) * scale
        s = s + F_blk[..., None] - F[:, :, None, :]
        s = jnp.where(kpos[None, :] <= q_pos[:, None], s, NEG_INF)
        p = jax.nn.softmax(s, axis=-1).astype(v.dtype)
        return jnp.einsum('bhqk,bhkd->bhqd', p, v)

    o = lax.map(block, (qb, Fb, pb))
    o = o.transpose(1, 0, 3, 2, 4).reshape(B, S, D)
    return o @ w_out


def _stick_breaking_mixer(h, w_in, w_out):
    B, S, D = h.shape
    H = SB_HEADS
    scale = HEAD_DIM ** -0.5
    q, k, v = jnp.split(h @ w_in, 3, axis=-1)
    to_heads = lambda t: t.reshape(B, S, H, HEAD_DIM).transpose(0, 2, 1, 3)
    q, k, v = to_heads(q), to_heads(k), to_heads(v)
    nb = S // Q_BLOCK
    kpos = jnp.arange(S)
    qb = q.reshape(B, H, nb, Q_BLOCK, HEAD_DIM).transpose(2, 0, 1, 3, 4)
    pb = kpos.reshape(nb, Q_BLOCK)

    def block(args):
        q_blk, q_pos = args
        z = jnp.einsum('bhqd,bhkd->bhqk', q_blk, k).astype(jnp.float32) * scale
        mask = kpos[None, :] < q_pos[:, None]
        log_beta = jax.nn.log_sigmoid(z)
        log_1m = jnp.where(mask, jax.nn.log_sigmoid(-z), 0.0)
        rest = lax.cumsum(log_1m, axis=3, reverse=True) - log_1m
        a = jnp.where(mask, jnp.exp(log_beta + rest), 0.0).astype(v.dtype)
        return jnp.einsum('bhqk,bhkd->bhqd', a, v)

    o = lax.map(block, (qb, pb))
    o = o.transpose(1, 0, 3, 2, 4).reshape(B, S, D)
    return o @ w_out


def _conformer_conv_mixer(h, w_in, b_in, dw, dw_b, ln_g, ln_b, w_out, b_out):
    D = h.shape[-1]
    a, g = jnp.split(h @ w_in + b_in, 2, axis=-1)
    y = a * jax.nn.sigmoid(g)
    y = lax.conv_general_dilated(
        y, dw[:, None, :], window_strides=(1,), padding=[(CONV_WIDTH - 1, 0)],
        dimension_numbers=('NWC', 'WIO', 'NWC'), feature_group_count=D) + dw_b
    y = jax.nn.silu(_layer_norm(y, ln_g, ln_b))
    return y @ w_out + b_out


def _swiglu(h, w_in, w_out):
    g, u = jnp.split(h @ w_in, 2, axis=-1)
    return (jax.nn.silu(g) * u) @ w_out


def setup_inputs(seed: int = 0) -> dict:
    key = jax.random.key(seed)
    ks = iter(jax.random.split(key, 40))
    D = D_MODEL
    f32 = jnp.float32
    nrm = lambda shape, s: jax.random.normal(next(ks), shape, f32) * s
    gain = lambda shape: 1.0 + nrm(shape, 0.02)
    return {
        "x": nrm((BATCH, SEQ, D), 1.0),
        "c": nrm((BATCH, D), 1.0),
        "mod_w": nrm((DEPTH, D, 6 * D), 0.1 * D ** -0.5),
        "mod_b": nrm((DEPTH, 6 * D), 0.01),
        "ln1_g": gain((DEPTH, D)),
        "ln1_b": nrm((DEPTH, D), 0.02),
        "ln2_g": gain((DEPTH, D)),
        "ln2_b": nrm((DEPTH, D), 0.02),
        "ffn_w_in": nrm((DEPTH, D, 2 * FFN_HIDDEN), D ** -0.5),
        "ffn_w_out": nrm((DEPTH, FFN_HIDDEN, D), FFN_HIDDEN ** -0.5 * DN_BETA),
        "gm_w_in": nrm((N_GM, D, 2 * GM_WIDTH), D ** -0.5),
        "gm_b_in": nrm((N_GM, 2 * GM_WIDTH), 0.02),
        "gm_ln_g": gain((N_GM, GM_WIDTH)),
        "gm_ln_b": nrm((N_GM, GM_WIDTH), 0.02),
        "gm_w_s": nrm((N_GM, GM_GROUPS, GM_CHUNK, GM_CHUNK), 0.5 * GM_CHUNK ** -0.5),
        "gm_b_s": gain((N_GM, GM_GROUPS, GM_CHUNK)),
        "gm_w_out": nrm((N_GM, GM_WIDTH, D), GM_WIDTH ** -0.5 * DN_BETA),
        "fox_w_in": nrm((N_FOX, D, 3 * D + FOX_HEADS), D ** -0.5),
        "fox_b_f": jax.random.uniform(next(ks), (N_FOX, FOX_HEADS), f32, 1.0, 4.0),
        "fox_w_out": nrm((N_FOX, D, D), D ** -0.5 * DN_BETA),
        "sb_w_in": nrm((N_SB, D, 3 * D), D ** -0.5),
        "sb_w_out": nrm((N_SB, D, D), D ** -0.5 * DN_BETA),
        "cv_w_in": nrm((N_CV, D, 2 * D), D ** -0.5),
        "cv_b_in": nrm((N_CV, 2 * D), 0.02),
        "cv_dw": nrm((N_CV, CONV_WIDTH, D), CONV_WIDTH ** -0.5),
        "cv_dw_b": nrm((N_CV, D), 0.02),
        "cv_ln_g": gain((N_CV, D)),
        "cv_ln_b": nrm((N_CV, D), 0.02),
        "cv_w_out": nrm((N_CV, D, D), D ** -0.5 * DN_BETA),
        "cv_b_out": nrm((N_CV, D), 0.02),
    }


def reference(x, c, mod_w, mod_b, ln1_g, ln1_b, ln2_g, ln2_b, ffn_w_in, ffn_w_out,
              gm_w_in, gm_b_in, gm_ln_g, gm_ln_b, gm_w_s, gm_b_s, gm_w_out,
              fox_w_in, fox_b_f, fox_w_out,
              sb_w_in, sb_w_out,
              cv_w_in, cv_b_in, cv_dw, cv_dw_b, cv_ln_g, cv_ln_b, cv_w_out, cv_b_out):
    c_act = jax.nn.silu(c)
    for l in range(DEPTH):
        m, j = l % N_MIXERS, l // N_MIXERS
        mod = c_act @ mod_w[l] + mod_b[l]
        sh1, sc1, g1, sh2, sc2, g2 = [t[:, None, :] for t in jnp.split(mod, 6, axis=-1)]
        h = x * (1.0 + sc1) + sh1
        if m == 0:
            y = _gmlp_mixer(h, gm_w_in[j], gm_b_in[j], gm_ln_g[j], gm_ln_b[j],
                            gm_w_s[j], gm_b_s[j], gm_w_out[j])
        elif m == 1:
            y = _fox_mixer(h, fox_w_in[j], fox_b_f[j], fox_w_out[j])
        elif m == 2:
            y = _stick_breaking_mixer(h, sb_w_in[j], sb_w_out[j])
        else:
            y = _conformer_conv_mixer(h, cv_w_in[j], cv_b_in[j], cv_dw[j], cv_dw_b[j],
                                      cv_ln_g[j], cv_ln_b[j], cv_w_out[j], cv_b_out[j])
        x = _layer_norm(DN_ALPHA * x + (1.0 + g1) * y, ln1_g[l], ln1_b[l])
        h = x * (1.0 + sc2) + sh2
        y = _swiglu(h, ffn_w_in[l], ffn_w_out[l])
        x = _layer_norm(DN_ALPHA * x + (1.0 + g2) * y, ln2_g[l], ln2_b[l])
    return x
```

```python
import functools

import jax
import jax.numpy as jnp
from jax import lax
from jax.experimental import pallas as pl
from jax.experimental.pallas import tpu as pltpu

f32 = jnp.float32
bf16 = jnp.bfloat16

D_MODEL = 1024
DEPTH = 4
HEAD_DIM = 64
N_HEADS = D_MODEL // HEAD_DIM
HEADS_PER_STEP = 2
LANES = 128
GM_CHUNK = 128
GM_GROUPS = 8
GM_GROUP_DIM = D_MODEL // GM_GROUPS
CONV_WIDTH = 31
CONV_HALO = 32
FFN_HIDDEN = 2816
DN_ALPHA = (2.0 * DEPTH) ** 0.25
LN_EPS = 1e-5
NEG_INF = -1e30
ATT_SCALE = HEAD_DIM ** -0.5

TM = 256
TQ = 256
TK = 256
VMEM_LIMIT = 56 * 1024 * 1024

_NT = (((1,), (1,)), ((), ()))


def _ln(x, g, b):
    mu = jnp.mean(x, axis=-1, keepdims=True)
    xc = x - mu
    var = jnp.mean(xc * xc, axis=-1, keepdims=True)
    return xc * lax.rsqrt(var + LN_EPS) * g + b


def _sigmoid(x):
    return 1.0 / (1.0 + jnp.exp(-x))


def _log_sigmoid(x):
    return jnp.minimum(x, 0.0) - jnp.log(1.0 + jnp.exp(-jnp.abs(x)))


def _gelu(x):
    return 0.5 * x * (1.0 + lax.erf(x * (2.0 ** -0.5)))


def _params(*sem):
    return pltpu.CompilerParams(dimension_semantics=sem, vmem_limit_bytes=VMEM_LIMIT)


def _const_spec(shape):
    nd = len(shape)
    return pl.BlockSpec(shape, lambda *_: (0,) * nd, pipeline_mode=pl.Buffered(1))


def _mod_kernel(c_ref, w_ref, b_ref, o_ref):
    c = c_ref[...]
    ca = (c * _sigmoid(c)).astype(bf16)
    o_ref[0, 0] = jnp.dot(ca, w_ref[0].astype(bf16), preferred_element_type=f32) + b_ref[0, 0]


def _modulation(c, mod_w, mod_b):
    B, D = c.shape
    L = mod_w.shape[0]
    out = pl.pallas_call(
        _mod_kernel,
        grid=(L, 6),
        in_specs=[
            pl.BlockSpec((B, D), lambda l, j: (0, 0)),
            pl.BlockSpec((1, D, D), lambda l, j: (l, 0, j)),
            pl.BlockSpec((1, 1, 1, D), lambda l, j: (l, j, 0, 0)),
        ],
        out_specs=pl.BlockSpec((1, 1, B, D), lambda l, j: (l, j, 0, 0)),
        out_shape=jax.ShapeDtypeStruct((L, 6, B, D), f32),
        compiler_params=_params("arbitrary", "arbitrary"),
        name="adaln_mod",
    )(c, mod_w, mod_b.reshape(L, 6, 1, D))
    return out.transpose(0, 2, 1, 3)


def _post_kernel(x_ref, y_ref, mod_ref, wmo_ref, bmo_ref, ln1g_ref, ln1b_ref, win_ref, wout_ref,
                 ln2g_ref, ln2b_ref, o_ref):
    x = x_ref[0]
    mod = mod_ref[0, 0]
    g1, sh2, sc2, g2 = mod[2:3], mod[3:4], mod[4:5], mod[5:6]
    y = jnp.dot(y_ref[0], wmo_ref[...], preferred_element_type=f32) + bmo_ref[...]
    x1 = _ln(DN_ALPHA * x + (1.0 + g1) * y, ln1g_ref[...], ln1b_ref[...])
    h = (x1 * (1.0 + sc2) + sh2).astype(bf16)
    gu = jnp.dot(h, win_ref[...], preferred_element_type=f32)
    g = gu[:, :FFN_HIDDEN]
    u = gu[:, FFN_HIDDEN:]
    a = (g * _sigmoid(g) * u).astype(bf16)
    y2 = jnp.dot(a, wout_ref[...], preferred_element_type=f32)
    o_ref[0] = _ln(DN_ALPHA * x1 + (1.0 + g2) * y2, ln2g_ref[...], ln2b_ref[...])


def _post(x, ypre, mod_l, w_mix_out, b_mix_out, ln1g, ln1b, w_in, w_out, ln2g, ln2b):
    B, S, D = x.shape
    row = lambda b, i: (b, i, 0)
    return pl.pallas_call(
        _post_kernel,
        grid=(B, S // TM),
        in_specs=[
            pl.BlockSpec((1, TM, D), row),
            pl.BlockSpec((1, TM, D), row),
            pl.BlockSpec((1, 1, 6, D), lambda b, i: (b, 0, 0, 0)),
            _const_spec((D, D)),
            _const_spec((1, D)),
            _const_spec((1, D)),
            _const_spec((1, D)),
            _const_spec((D, 2 * FFN_HIDDEN)),
            _const_spec((FFN_HIDDEN, D)),
            _const_spec((1, D)),
            _const_spec((1, D)),
        ],
        out_specs=pl.BlockSpec((1, TM, D), row),
        out_shape=jax.ShapeDtypeStruct((B, S, D), f32),
        compiler_params=_params("arbitrary", "arbitrary"),
        name="mixout_ln_ffn_ln",
    )(x, ypre, mod_l.reshape(B, 1, 6, D), w_mix_out, b_mix_out, ln1g, ln1b, w_in, w_out, ln2g, ln2b)


def _gmlp_kernel(x_ref, mod_ref, win_ref, bin_ref, lng_ref, lnb_ref, ws_ref, bs_ref, o_ref):
    h = _modulated(x_ref, mod_ref)
    z = _gelu(jnp.dot(h, win_ref[...], preferred_element_type=f32) + bin_ref[...])
    u = z[:, :D_MODEL]
    v = _ln(z[:, D_MODEL:], lng_ref[...], lnb_ref[...]).astype(bf16)
    r = lax.broadcasted_iota(jnp.int32, (GM_CHUNK, GM_CHUNK), 0)
    c = lax.broadcasted_iota(jnp.int32, (GM_CHUNK, GM_CHUNK), 1)
    for g in range(GM_GROUPS):
        cols = slice(g * GM_GROUP_DIM, (g + 1) * GM_GROUP_DIM)
        wm = jnp.where(r >= c, ws_ref[g], 0.0).astype(bf16)
        for n in range(TM // GM_CHUNK):
            rows = slice(n * GM_CHUNK, (n + 1) * GM_CHUNK)
            sv = jnp.dot(wm, v[rows, cols], preferred_element_type=f32) + bs_ref[:, cols]
            o_ref[0, rows, cols] = (u[rows, cols] * sv).astype(bf16)


def _gmlp_pre(x, mod_l, w_in, b_in, ln_g, ln_b, w_s, bs_full):
    B, S, D = x.shape
    row = lambda b, i: (b, i, 0)
    return pl.pallas_call(
        _gmlp_kernel,
        grid=(B, S // TM),
        in_specs=[
            pl.BlockSpec((1, TM, D), row),
            pl.BlockSpec((1, 1, 6, D), lambda b, i: (b, 0, 0, 0)),
            _const_spec((D, 2 * D)),
            _const_spec((1, 2 * D)),
            _const_spec((1, D)),
            _const_spec((1, D)),
            _const_spec((GM_GROUPS, GM_CHUNK, GM_CHUNK)),
            _const_spec((GM_CHUNK, D)),
        ],
        out_specs=pl.BlockSpec((1, TM, D), row),
        out_shape=jax.ShapeDtypeStruct((B, S, D), bf16),
        compiler_params=_params("arbitrary", "arbitrary"),
        name="gmlp_pre",
    )(x, mod_l.reshape(B, 1, 6, D), w_in, b_in, ln_g, ln_b, w_s, bs_full)


def _modulated(x_ref, mod_ref):
    mod = mod_ref[0, 0]
    return (x_ref[0] * (1.0 + mod[1:2]) + mod[0:1]).astype(bf16)


def _qkv_store(h, w_ref, q_ref, k_ref, vt_ref):
    qkv = jnp.dot(h, w_ref[...], preferred_element_type=f32)
    q_ref[0] = (qkv[:, :D_MODEL] * ATT_SCALE).astype(bf16)
    k_ref[0] = qkv[:, D_MODEL:2 * D_MODEL].astype(bf16)
    vt_ref[0, 0] = qkv[:, 2 * D_MODEL:].T.astype(bf16)


def _qkv_kernel(x_ref, mod_ref, w_ref, q_ref, k_ref, vt_ref):
    _qkv_store(_modulated(x_ref, mod_ref), w_ref, q_ref, k_ref, vt_ref)


def _qkvf_kernel(x_ref, mod_ref, w_ref, wf_ref, bf_ref, q_ref, k_ref, vt_ref, lf_ref):
    h = _modulated(x_ref, mod_ref)
    _qkv_store(h, w_ref, q_ref, k_ref, vt_ref)
    f = jnp.dot(h, wf_ref[...], preferred_element_type=f32)
    ft = f.T[:N_HEADS] + bf_ref[...]
    lf_ref[0] = _log_sigmoid(ft)


def _attn_in(x, mod_l, w_qkv, w_f=None, b_f=None):
    B, S, D = x.shape
    row = lambda b, i: (b, i, 0)
    in_specs = [
        pl.BlockSpec((1, TK, D), row),
        pl.BlockSpec((1, 1, 6, D), lambda b, i: (b, 0, 0, 0)),
        _const_spec((D, 3 * D)),
    ]
    out_specs = [
        pl.BlockSpec((1, TK, D), row),
        pl.BlockSpec((1, TK, D), row),
        pl.BlockSpec((1, 1, D, TK), lambda b, i: (b, i, 0, 0)),
    ]
    out_shape = [
        jax.ShapeDtypeStruct((B, S, D), bf16),
        jax.ShapeDtypeStruct((B, S, D), bf16),
        jax.ShapeDtypeStruct((B, S // TK, D, TK), bf16),
    ]
    args = [x, mod_l.reshape(B, 1, 6, D), w_qkv]
    if w_f is None:
        body, name = _qkv_kernel, "attn_qkv"
    else:
        body, name = _qkvf_kernel, "attn_qkv_forget"
        in_specs += [_const_spec((D, LANES)), _const_spec((N_HEADS, 1))]
        out_specs += [pl.BlockSpec((1, N_HEADS, TK), lambda b, i: (b, 0, i))]
        out_shape += [jax.ShapeDtypeStruct((B, N_HEADS, S), f32)]
        args += [w_f, b_f]
    return pl.pallas_call(
        body,
        grid=(B, S // TK),
        in_specs=in_specs,
        out_specs=out_specs,
        out_shape=out_shape,
        compiler_params=_params("arbitrary", "arbitrary"),
        name=name,
    )(*args)


def _cumsum_kernel(lf_ref, f_ref):
    S = lf_ref.shape[2]
    r = lax.broadcasted_iota(jnp.int32, (LANES, LANES), 0)
    c = lax.broadcasted_iota(jnp.int32, (LANES, LANES), 1)
    upper = (r <= c).astype(f32)
    carry = jnp.zeros((N_HEADS, 1), f32)
    for j in range(S // LANES):
        cols = slice(j * LANES, (j + 1) * LANES)
        cs = jnp.dot(lf_ref[0, :, cols], upper, preferred_element_type=f32,
                     precision=lax.Precision.HIGHEST) + carry
        f_ref[0, :, cols] = cs
        carry = cs[:, LANES - 1:LANES]


def _cumsum(lf):
    B, H, S = lf.shape
    return pl.pallas_call(
        _cumsum_kernel,
        grid=(B,),
        in_specs=[pl.BlockSpec((1, H, S), lambda b: (b, 0, 0))],
        out_specs=pl.BlockSpec((1, H, S), lambda b: (b, 0, 0)),
        out_shape=jax.ShapeDtypeStruct((B, H, S), f32),
        compiler_params=_params("arbitrary"),
        name="forget_cumsum",
    )(lf)


def _head_mask(q2, hh):
    lane = lax.broadcasted_iota(jnp.int32, q2.shape, 1)
    return jnp.where((lane // HEAD_DIM) == hh, q2, jnp.zeros_like(q2))


def _key_block(k_ref, j):
    return k_ref[0, pl.ds(pl.multiple_of(j * TK, TK), TK), :]


def _fox_attn_kernel(q_ref, k_ref, vt_ref, fq_ref, fk_ref, o_ref, fcol_ref):
    i = pl.program_id(2)
    n_kb = fk_ref.shape[3] // TK

    @pl.when(i == 0)
    def _():
        for hh in range(HEADS_PER_STEP):
            for j in range(n_kb):
                rowv = fk_ref[0, 0, hh:hh + 1, j * TK:(j + 1) * TK]
                fcol_ref[hh, j * TK:(j + 1) * TK, :] = jnp.broadcast_to(rowv, (LANES, TK)).T

    q2 = q_ref[0]
    kpos = lax.broadcasted_iota(jnp.int32, (TK, TQ), 0)
    qpos = lax.broadcasted_iota(jnp.int32, (TK, TQ), 1)
    outs = []
    for hh in range(HEADS_PER_STEP):
        qh = _head_mask(q2, hh)
        frow = fq_ref[0, 0, hh:hh + 1, :]

        def step(j, carry, diag, qh=qh, frow=frow, hh=hh):
            m, l, acc = carry
            s = lax.dot_general(_key_block(k_ref, j), qh, _NT, preferred_element_type=f32)
            fk = fcol_ref[hh, pl.ds(pl.multiple_of(j * TK, TK), TK), :]
            s = (s + frow) - jnp.concatenate([fk] * (TQ // LANES), axis=1)
            if diag:
                s = jnp.where(kpos <= qpos, s, NEG_INF)
            m_new = jnp.maximum(m, jnp.max(s, axis=0, keepdims=True))
            alpha = jnp.exp(m - m_new)
            p = jnp.exp(s - m_new)
            l = alpha * l + jnp.sum(p, axis=0, keepdims=True)
            vt = vt_ref[0, j, hh * HEAD_DIM:(hh + 1) * HEAD_DIM, :]
            acc = alpha * acc + jnp.dot(vt, p.astype(bf16), preferred_element_type=f32)
            return m_new, l, acc

        init = (jnp.full((1, TQ), NEG_INF, f32), jnp.zeros((1, TQ), f32), jnp.zeros((HEAD_DIM, TQ), f32))
        carry = lax.fori_loop(0, i, functools.partial(step, diag=False), init)
        m, l, acc = step(i, carry, diag=True)
        outs.append((acc * (1.0 / l)).T)
    o_ref[0] = jnp.concatenate(outs, axis=1).astype(bf16)


def _fox_attn(q, k, vt, F):
    B, S, D = q.shape
    n_hp = N_HEADS // HEADS_PER_STEP
    F4 = F.reshape(B, n_hp, HEADS_PER_STEP, S)
    return pl.pallas_call(
        _fox_attn_kernel,
        grid=(B, n_hp, S // TQ),
        in_specs=[
            pl.BlockSpec((1, TQ, LANES), lambda b, h, i: (b, i, h)),
            pl.BlockSpec((1, S, LANES), lambda b, h, i: (b, 0, h)),
            pl.BlockSpec((1, S // TK, LANES, TK), lambda b, h, i: (b, 0, h, 0)),
            pl.BlockSpec((1, 1, HEADS_PER_STEP, TQ), lambda b, h, i: (b, h, 0, i)),
            pl.BlockSpec((1, 1, HEADS_PER_STEP, S), lambda b, h, i: (b, h, 0, 0)),
        ],
        out_specs=pl.BlockSpec((1, TQ, LANES), lambda b, h, i: (b, i, h)),
        out_shape=jax.ShapeDtypeStruct((B, S, D), bf16),
        scratch_shapes=[pltpu.VMEM((HEADS_PER_STEP, S, LANES), f32)],
        compiler_params=_params("arbitrary", "arbitrary", "arbitrary"),
        name="fox_attention",
    )(q, k, vt, F4, F4)


def _sb_attn_kernel(q_ref, k_ref, vt_ref, o_ref):
    i = pl.program_id(2)
    q2 = q_ref[0]
    kpos = lax.broadcasted_iota(jnp.int32, (TK, TQ), 0)
    qpos = lax.broadcasted_iota(jnp.int32, (TK, TQ), 1)
    r = lax.broadcasted_iota(jnp.int32, (TK, TK), 0)
    c = lax.broadcasted_iota(jnp.int32, (TK, TK), 1)
    suffix = (r <= c).astype(bf16)
    outs = []
    for hh in range(HEADS_PER_STEP):
        qh = _head_mask(q2, hh)

        def step(j, carry, diag, qh=qh, hh=hh):
            tail, acc = carry
            z = lax.dot_general(_key_block(k_ref, j), qh, _NT, preferred_element_type=f32)
            sp = jnp.log(1.0 + jnp.exp(-jnp.abs(z)))
            log_beta = jnp.minimum(z, 0.0) - sp
            log_1m = -jnp.maximum(z, 0.0) - sp
            if diag:
                log_1m = jnp.where(kpos < qpos, log_1m, 0.0)
            hi = log_1m.astype(bf16)
            lo = (log_1m - hi.astype(f32)).astype(bf16)
            incl = (jnp.dot(suffix, hi, preferred_element_type=f32)
                    + jnp.dot(suffix, lo, preferred_element_type=f32))
            a = jnp.exp(log_beta + ((incl - log_1m) + tail))
            if diag:
                a = jnp.where(kpos < qpos, a, 0.0)
            vt = vt_ref[0, j, hh * HEAD_DIM:(hh + 1) * HEAD_DIM, :]
            acc = acc + jnp.dot(vt, a.astype(bf16), preferred_element_type=f32)
            return tail + incl[0:1, :], acc

        init = (jnp.zeros((1, TQ), f32), jnp.zeros((HEAD_DIM, TQ), f32))
        carry = step(i, init, diag=True)
        _, acc = lax.fori_loop(0, i, lambda t, cr: step(i - 1 - t, cr, diag=False), carry)
        outs.append(acc.T)
    o_ref[0] = jnp.concatenate(outs, axis=1).astype(bf16)


def _sb_attn(q, k, vt):
    B, S, D = q.shape
    n_hp = N_HEADS // HEADS_PER_STEP
    return pl.pallas_call(
        _sb_attn_kernel,
        grid=(B, n_hp, S // TQ),
        in_specs=[
            pl.BlockSpec((1, TQ, LANES), lambda b, h, i: (b, i, h)),
            pl.BlockSpec((1, S, LANES), lambda b, h, i: (b, 0, h)),
            pl.BlockSpec((1, S // TK, LANES, TK), lambda b, h, i: (b, 0, h, 0)),
        ],
        out_specs=pl.BlockSpec((1, TQ, LANES), lambda b, h, i: (b, i, h)),
        out_shape=jax.ShapeDtypeStruct((B, S, D), bf16),
        compiler_params=_params("arbitrary", "arbitrary", "arbitrary"),
        name="stickbreak_attention",
    )(q, k, vt)


def _glu_kernel(x_ref, mod_ref, w_ref, b_ref, o_ref):
    h = _modulated(x_ref, mod_ref)
    ag = jnp.dot(h, w_ref[...], preferred_element_type=f32) + b_ref[...]
    o_ref[0] = ag[:, :D_MODEL] * _sigmoid(ag[:, D_MODEL:])


def _glu(x, mod_l, w_in, b_in):
    B, S, D = x.shape
    row = lambda b, i: (b, i, 0)
    return pl.pallas_call(
        _glu_kernel,
        grid=(B, S // TM),
        in_specs=[
            pl.BlockSpec((1, TM, D), row),
            pl.BlockSpec((1, 1, 6, D), lambda b, i: (b, 0, 0, 0)),
            _const_spec((D, 2 * D)),
            _const_spec((1, 2 * D)),
        ],
        out_specs=pl.BlockSpec((1, TM, D), row),
        out_shape=jax.ShapeDtypeStruct((B, S, D), f32),
        compiler_params=_params("arbitrary", "arbitrary"),
        name="conv_glu",
    )(x, mod_l.reshape(B, 1, 6, D), w_in, b_in)


def _dwconv_kernel(halo_ref, y_ref, dw_ref, dwb_ref, lng_ref, lnb_ref, o_ref, ybuf_ref):
    i = pl.program_id(1)
    halo = halo_ref[0]
    ybuf_ref[0:CONV_HALO, :] = jnp.where(i > 0, halo, jnp.zeros_like(halo))
    ybuf_ref[CONV_HALO:, :] = y_ref[0]
    first = CONV_HALO - (CONV_WIDTH - 1)
    for cb in range(D_MODEL // LANES):
        cols = slice(cb * LANES, (cb + 1) * LANES)
        acc = jnp.zeros((TM, LANES), f32)
        for t in range(CONV_WIDTH):
            acc = acc + ybuf_ref[first + t:first + t + TM, cols] * dw_ref[t:t + 1, cols]
        ybuf_ref[CONV_HALO:, cols] = acc + dwb_ref[:, cols]
    yn = _ln(ybuf_ref[CONV_HALO:, :], lng_ref[...], lnb_ref[...])
    o_ref[0] = (yn * _sigmoid(yn)).astype(bf16)


def _dwconv(y, dw, dw_b, ln_g, ln_b):
    B, S, D = y.shape
    per = TM // CONV_HALO
    row = lambda b, i: (b, i, 0)
    return pl.pallas_call(
        _dwconv_kernel,
        grid=(B, S // TM),
        in_specs=[
            pl.BlockSpec((1, CONV_HALO, D), lambda b, i: (b, jnp.maximum(i * per - 1, 0), 0)),
            pl.BlockSpec((1, TM, D), row),
            _const_spec((CONV_WIDTH, D)),
            _const_spec((1, D)),
            _const_spec((1, D)),
            _const_spec((1, D)),
        ],
        out_specs=pl.BlockSpec((1, TM, D), row),
        out_shape=jax.ShapeDtypeStruct((B, S, D), bf16),
        scratch_shapes=[pltpu.VMEM((CONV_HALO + TM, D), f32)],
        compiler_params=_params("arbitrary", "arbitrary"),
        name="conv_dw_ln_swish",
    )(y, y, dw, dw_b, ln_g, ln_b)


def kernel(x, c, mod_w, mod_b, ln1_g, ln1_b, ln2_g, ln2_b, ffn_w_in, ffn_w_out, gm_w_in, gm_b_in, gm_ln_g, gm_ln_b, gm_w_s, gm_b_s, gm_w_out, fox_w_in, fox_b_f, fox_w_out, sb_w_in, sb_w_out, cv_w_in, cv_b_in, cv_dw, cv_dw_b, cv_ln_g, cv_ln_b, cv_w_out, cv_b_out):
    B, S, D = x.shape
    assert (B, S, D) == (16, 4096, D_MODEL) and S % TM == 0 and TQ == TK
    row = lambda v: v.reshape(1, -1)
    mod = _modulation(c, mod_w, mod_b)
    zero_bias = jnp.zeros((1, D), f32)
    for l in range(DEPTH):
        m, j = l % 4, l // 4
        if m == 0:
            bs_full = jnp.repeat(gm_b_s[j].T, GM_GROUP_DIM, axis=1)
            ypre = _gmlp_pre(x, mod[l], gm_w_in[j].astype(bf16), row(gm_b_in[j]), row(gm_ln_g[j]),
                             row(gm_ln_b[j]), gm_w_s[j], bs_full)
            w_mo, b_mo = gm_w_out[j], zero_bias
        elif m == 1:
            w = fox_w_in[j]
            w_f = jnp.pad(w[:, 3 * D:], ((0, 0), (0, LANES - N_HEADS))).astype(bf16)
            q, k, vt, lf = _attn_in(x, mod[l], w[:, :3 * D].astype(bf16), w_f, fox_b_f[j].reshape(N_HEADS, 1))
            ypre = _fox_attn(q, k, vt, _cumsum(lf))
            w_mo, b_mo = fox_w_out[j], zero_bias
        elif m == 2:
            q, k, vt = _attn_in(x, mod[l], sb_w_in[j].astype(bf16))
            ypre = _sb_attn(q, k, vt)
            w_mo, b_mo = sb_w_out[j], zero_bias
        else:
            y = _glu(x, mod[l], cv_w_in[j].astype(bf16), row(cv_b_in[j]))
            ypre = _dwconv(y, cv_dw[j], row(cv_dw_b[j]), row(cv_ln_g[j]), row(cv_ln_b[j]))
            w_mo, b_mo = cv_w_out[j], row(cv_b_out[j])
        x = _post(x, ypre, mod[l], w_mo.astype(bf16), b_mo, row(ln1_g[l]), row(ln1_b[l]),
                  ffn_w_in[l].astype(bf16), ffn_w_out[l].astype(bf16), row(ln2_g[l]), row(ln2_b[l]))
    return x
```

```python
import functools

import jax
import jax.numpy as jnp
from jax import lax
from jax.experimental import pallas as pl
from jax.experimental.pallas import tpu as pltpu

f32 = jnp.float32
bf16 = jnp.bfloat16

D_MODEL = 1024
DEPTH = 4
HEAD_DIM = 64
N_HEADS = D_MODEL // HEAD_DIM
HEADS_PER_STEP = 2
LANES = 128
GM_CHUNK = 128
GM_GROUPS = 8
GM_GROUP_DIM = D_MODEL // GM_GROUPS
CONV_WIDTH = 31
CONV_HALO = 32
FFN_HIDDEN = 2816
DN_ALPHA = (2.0 * DEPTH) ** 0.25
LN_EPS = 1e-5
NEG_INF = -1e30
ATT_SCALE = HEAD_DIM ** -0.5

TM = 256
TQ = 512
TK = 256
VMEM_LIMIT = 56 * 1024 * 1024

_NT = (((1,), (1,)), ((), ()))


def _ln(x, g, b):
    mu = jnp.mean(x, axis=-1, keepdims=True)
    xc = x - mu
    var = jnp.mean(xc * xc, axis=-1, keepdims=True)
    return xc * lax.rsqrt(var + LN_EPS) * g + b


def _sigmoid(x):
    return 1.0 / (1.0 + jnp.exp(-x))


def _log_sigmoid(x):
    return jnp.minimum(x, 0.0) - jnp.log(1.0 + jnp.exp(-jnp.abs(x)))


def _gelu(x):
    return 0.5 * x * (1.0 + lax.erf(x * (2.0 ** -0.5)))


def _params(*sem):
    return pltpu.CompilerParams(dimension_semantics=sem, vmem_limit_bytes=VMEM_LIMIT)


def _const_spec(shape):
    nd = len(shape)
    return pl.BlockSpec(shape, lambda *_: (0,) * nd, pipeline_mode=pl.Buffered(1))


def _mod_kernel(c_ref, w_ref, b_ref, o_ref):
    c = c_ref[...]
    ca = (c * _sigmoid(c)).astype(bf16)
    o_ref[0, 0] = jnp.dot(ca, w_ref[0].astype(bf16), preferred_element_type=f32) + b_ref[0, 0]


def _modulation(c, mod_w, mod_b):
    B, D = c.shape
    L = mod_w.shape[0]
    out = pl.pallas_call(
        _mod_kernel,
        grid=(L, 6),
        in_specs=[
            pl.BlockSpec((B, D), lambda l, j: (0, 0)),
            pl.BlockSpec((1, D, D), lambda l, j: (l, 0, j)),
            pl.BlockSpec((1, 1, 1, D), lambda l, j: (l, j, 0, 0)),
        ],
        out_specs=pl.BlockSpec((1, 1, B, D), lambda l, j: (l, j, 0, 0)),
        out_shape=jax.ShapeDtypeStruct((L, 6, B, D), f32),
        compiler_params=_params("arbitrary", "arbitrary"),
        name="adaln_mod",
    )(c, mod_w, mod_b.reshape(L, 6, 1, D))
    return out.transpose(0, 2, 1, 3)


def _post_kernel(x_ref, y_ref, mod_ref, wmo_ref, bmo_ref, ln1g_ref, ln1b_ref, win_ref, wout_ref,
                 ln2g_ref, ln2b_ref, o_ref):
    x = x_ref[0]
    mod = mod_ref[0, 0]
    g1, sh2, sc2, g2 = mod[2:3], mod[3:4], mod[4:5], mod[5:6]
    y = jnp.dot(y_ref[0], wmo_ref[...], preferred_element_type=f32) + bmo_ref[...]
    x1 = _ln(DN_ALPHA * x + (1.0 + g1) * y, ln1g_ref[...], ln1b_ref[...])
    h = (x1 * (1.0 + sc2) + sh2).astype(bf16)
    gu = jnp.dot(h, win_ref[...], preferred_element_type=f32)
    g = gu[:, :FFN_HIDDEN]
    u = gu[:, FFN_HIDDEN:]
    a = (g * _sigmoid(g) * u).astype(bf16)
    y2 = jnp.dot(a, wout_ref[...], preferred_element_type=f32)
    o_ref[0] = _ln(DN_ALPHA * x1 + (1.0 + g2) * y2, ln2g_ref[...], ln2b_ref[...])


def _post(x, ypre, mod_l, w_mix_out, b_mix_out, ln1g, ln1b, w_in, w_out, ln2g, ln2b):
    B, S, D = x.shape
    row = lambda b, i: (b, i, 0)
    return pl.pallas_call(
        _post_kernel,
        grid=(B, S // TM),
        in_specs=[
            pl.BlockSpec((1, TM, D), row),
            pl.BlockSpec((1, TM, D), row),
            pl.BlockSpec((1, 1, 6, D), lambda b, i: (b, 0, 0, 0)),
            _const_spec((D, D)),
            _const_spec((1, D)),
            _const_spec((1, D)),
            _const_spec((1, D)),
            _const_spec((D, 2 * FFN_HIDDEN)),
            _const_spec((FFN_HIDDEN, D)),
            _const_spec((1, D)),
            _const_spec((1, D)),
        ],
        out_specs=pl.BlockSpec((1, TM, D), row),
        out_shape=jax.ShapeDtypeStruct((B, S, D), f32),
        compiler_params=_params("arbitrary", "arbitrary"),
        name="mixout_ln_ffn_ln",
    )(x, ypre, mod_l.reshape(B, 1, 6, D), w_mix_out, b_mix_out, ln1g, ln1b, w_in, w_out, ln2g, ln2b)


def _gmlp_kernel(x_ref, mod_ref, win_ref, bin_ref, lng_ref, lnb_ref, ws_ref, bs_ref, o_ref):
    h = _modulated(x_ref, mod_ref)
    z = _gelu(jnp.dot(h, win_ref[...], preferred_element_type=f32) + bin_ref[...])
    u = z[:, :D_MODEL]
    v = _ln(z[:, D_MODEL:], lng_ref[...], lnb_ref[...]).astype(bf16)
    r = lax.broadcasted_iota(jnp.int32, (GM_CHUNK, GM_CHUNK), 0)
    c = lax.broadcasted_iota(jnp.int32, (GM_CHUNK, GM_CHUNK), 1)
    for g in range(GM_GROUPS):
        cols = slice(g * GM_GROUP_DIM, (g + 1) * GM_GROUP_DIM)
        wm = jnp.where(r >= c, ws_ref[g], 0.0).astype(bf16)
        for n in range(TM // GM_CHUNK):
            rows = slice(n * GM_CHUNK, (n + 1) * GM_CHUNK)
            sv = jnp.dot(wm, v[rows, cols], preferred_element_type=f32) + bs_ref[:, cols]
            o_ref[0, rows, cols] = (u[rows, cols] * sv).astype(bf16)


def _gmlp_pre(x, mod_l, w_in, b_in, ln_g, ln_b, w_s, bs_full):
    B, S, D = x.shape
    row = lambda b, i: (b, i, 0)
    return pl.pallas_call(
        _gmlp_kernel,
        grid=(B, S // TM),
        in_specs=[
            pl.BlockSpec((1, TM, D), row),
            pl.BlockSpec((1, 1, 6, D), lambda b, i: (b, 0, 0, 0)),
            _const_spec((D, 2 * D)),
            _const_spec((1, 2 * D)),
            _const_spec((1, D)),
            _const_spec((1, D)),
            _const_spec((GM_GROUPS, GM_CHUNK, GM_CHUNK)),
            _const_spec((GM_CHUNK, D)),
        ],
        out_specs=pl.BlockSpec((1, TM, D), row),
        out_shape=jax.ShapeDtypeStruct((B, S, D), bf16),
        compiler_params=_params("arbitrary", "arbitrary"),
        name="gmlp_pre",
    )(x, mod_l.reshape(B, 1, 6, D), w_in, b_in, ln_g, ln_b, w_s, bs_full)


def _modulated(x_ref, mod_ref):
    mod = mod_ref[0, 0]
    return (x_ref[0] * (1.0 + mod[1:2]) + mod[0:1]).astype(bf16)


def _qkv_store(h, w_ref, q_ref, k_ref, vt_ref):
    qkv = jnp.dot(h, w_ref[...], preferred_element_type=f32)
    q_ref[0] = (qkv[:, :D_MODEL] * ATT_SCALE).astype(bf16)
    k_ref[0] = qkv[:, D_MODEL:2 * D_MODEL].astype(bf16)
    vt_ref[0, 0] = qkv[:, 2 * D_MODEL:].T.astype(bf16)


def _qkv_kernel(x_ref, mod_ref, w_ref, q_ref, k_ref, vt_ref):
    _qkv_store(_modulated(x_ref, mod_ref), w_ref, q_ref, k_ref, vt_ref)


def _qkvf_kernel(x_ref, mod_ref, w_ref, wf_ref, bf_ref, q_ref, k_ref, vt_ref, lf_ref):
    h = _modulated(x_ref, mod_ref)
    _qkv_store(h, w_ref, q_ref, k_ref, vt_ref)
    f = jnp.dot(h, wf_ref[...], preferred_element_type=f32)
    ft = f.T[:N_HEADS] + bf_ref[...]
    lf_ref[0] = _log_sigmoid(ft)


def _attn_in(x, mod_l, w_qkv, w_f=None, b_f=None):
    B, S, D = x.shape
    row = lambda b, i: (b, i, 0)
    in_specs = [
        pl.BlockSpec((1, TK, D), row),
        pl.BlockSpec((1, 1, 6, D), lambda b, i: (b, 0, 0, 0)),
        _const_spec((D, 3 * D)),
    ]
    out_specs = [
        pl.BlockSpec((1, TK, D), row),
        pl.BlockSpec((1, TK, D), row),
        pl.BlockSpec((1, 1, D, TK), lambda b, i: (b, i, 0, 0)),
    ]
    out_shape = [
        jax.ShapeDtypeStruct((B, S, D), bf16),
        jax.ShapeDtypeStruct((B, S, D), bf16),
        jax.ShapeDtypeStruct((B, S // TK, D, TK), bf16),
    ]
    args = [x, mod_l.reshape(B, 1, 6, D), w_qkv]
    if w_f is None:
        body, name = _qkv_kernel, "attn_qkv"
    else:
        body, name = _qkvf_kernel, "attn_qkv_forget"
        in_specs += [_const_spec((D, LANES)), _const_spec((N_HEADS, 1))]
        out_specs += [pl.BlockSpec((1, N_HEADS, TK), lambda b, i: (b, 0, i))]
        out_shape += [jax.ShapeDtypeStruct((B, N_HEADS, S), f32)]
        args += [w_f, b_f]
    return pl.pallas_call(
        body,
        grid=(B, S // TK),
        in_specs=in_specs,
        out_specs=out_specs,
        out_shape=out_shape,
        compiler_params=_params("arbitrary", "arbitrary"),
        name=name,
    )(*args)


def _cumsum_kernel(lf_ref, f_ref):
    S = lf_ref.shape[2]
    r = lax.broadcasted_iota(jnp.int32, (LANES, LANES), 0)
    c = lax.broadcasted_iota(jnp.int32, (LANES, LANES), 1)
    upper = (r <= c).astype(f32)
    carry = jnp.zeros((N_HEADS, 1), f32)
    for j in range(S // LANES):
        cols = slice(j * LANES, (j + 1) * LANES)
        cs = jnp.dot(lf_ref[0, :, cols], upper, preferred_element_type=f32,
                     precision=lax.Precision.HIGHEST) + carry
        f_ref[0, :, cols] = cs
        carry = cs[:, LANES - 1:LANES]


def _cumsum(lf):
    B, H, S = lf.shape
    return pl.pallas_call(
        _cumsum_kernel,
        grid=(B,),
        in_specs=[pl.BlockSpec((1, H, S), lambda b: (b, 0, 0))],
        out_specs=pl.BlockSpec((1, H, S), lambda b: (b, 0, 0)),
        out_shape=jax.ShapeDtypeStruct((B, H, S), f32),
        compiler_params=_params("arbitrary"),
        name="forget_cumsum",
    )(lf)


def _head_mask(q2, hh):
    lane = lax.broadcasted_iota(jnp.int32, q2.shape, 1)
    return jnp.where((lane // HEAD_DIM) == hh, q2, jnp.zeros_like(q2))


def _key_block(k_ref, j):
    return k_ref[0, pl.ds(pl.multiple_of(j * TK, TK), TK), :]


def _fox_attn_kernel(q_ref, k_ref, vt_ref, fq_ref, fk_ref, o_ref, fcol_ref):
    i = pl.program_id(2)
    n_kb = fk_ref.shape[3] // TK

    @pl.when(i == 0)
    def _():
        for hh in range(HEADS_PER_STEP):
            for j in range(n_kb):
                rowv = fk_ref[0, 0, hh:hh + 1, j * TK:(j + 1) * TK]
                fcol_ref[hh, j * TK:(j + 1) * TK, :] = jnp.broadcast_to(rowv, (LANES, TK)).T

    q2 = q_ref[0]
    qhs = [_head_mask(q2, hh) for hh in range(HEADS_PER_STEP)]
    frows = [fq_ref[0, 0, hh:hh + 1, :] for hh in range(HEADS_PER_STEP)]
    kpos = lax.broadcasted_iota(jnp.int32, (TK, TQ), 0)
    qpos = lax.broadcasted_iota(jnp.int32, (TK, TQ), 1) + i * TQ

    def step(j, carry, diag):
        kb = _key_block(k_ref, j)
        out = []
        for hh in range(HEADS_PER_STEP):
            m, l, acc = carry[hh]
            s = lax.dot_general(kb, qhs[hh], _NT, preferred_element_type=f32)
            fk = fcol_ref[hh, pl.ds(pl.multiple_of(j * TK, TK), TK), :]
            s = (s + frows[hh]) - jnp.concatenate([fk] * (TQ // LANES), axis=1)
            if diag:
                s = jnp.where(kpos + j * TK <= qpos, s, NEG_INF)
            m_new = jnp.maximum(m, jnp.max(s, axis=0, keepdims=True))
            alpha = jnp.exp(m - m_new)
            p = jnp.exp(s - m_new)
            l = alpha * l + jnp.sum(p, axis=0, keepdims=True)
            vt = vt_ref[0, j, hh * HEAD_DIM:(hh + 1) * HEAD_DIM, :]
            acc = alpha * acc + jnp.dot(vt, p.astype(bf16), preferred_element_type=f32)
            out.append((m_new, l, acc))
        return tuple(out)

    init = tuple((jnp.full((1, TQ), NEG_INF, f32), jnp.zeros((1, TQ), f32), jnp.zeros((HEAD_DIM, TQ), f32))
                 for _ in range(HEADS_PER_STEP))
    n_full = i * (TQ // TK)
    carry = lax.fori_loop(0, n_full, functools.partial(step, diag=False), init)
    for d in range(TQ // TK):
        carry = step(n_full + d, carry, diag=True)
    o_ref[0] = jnp.concatenate([(acc * (1.0 / l)).T for _, l, acc in carry], axis=1).astype(bf16)


def _fox_attn(q, k, vt, F):
    B, S, D = q.shape
    n_hp = N_HEADS // HEADS_PER_STEP
    F4 = F.reshape(B, n_hp, HEADS_PER_STEP, S)
    return pl.pallas_call(
        _fox_attn_kernel,
        grid=(B, n_hp, S // TQ),
        in_specs=[
            pl.BlockSpec((1, TQ, LANES), lambda b, h, i: (b, i, h)),
            pl.BlockSpec((1, S, LANES), lambda b, h, i: (b, 0, h)),
            pl.BlockSpec((1, S // TK, LANES, TK), lambda b, h, i: (b, 0, h, 0)),
            pl.BlockSpec((1, 1, HEADS_PER_STEP, TQ), lambda b, h, i: (b, h, 0, i)),
            pl.BlockSpec((1, 1, HEADS_PER_STEP, S), lambda b, h, i: (b, h, 0, 0)),
        ],
        out_specs=pl.BlockSpec((1, TQ, LANES), lambda b, h, i: (b, i, h)),
        out_shape=jax.ShapeDtypeStruct((B, S, D), bf16),
        scratch_shapes=[pltpu.VMEM((HEADS_PER_STEP, S, LANES), f32)],
        compiler_params=_params("arbitrary", "arbitrary", "arbitrary"),
        name="fox_attention",
    )(q, k, vt, F4, F4)


def _sb_attn_kernel(q_ref, k_ref, vt_ref, o_ref):
    i = pl.program_id(2)
    q2 = q_ref[0]
    qhs = [_head_mask(q2, hh) for hh in range(HEADS_PER_STEP)]
    kpos = lax.broadcasted_iota(jnp.int32, (TK, TQ), 0)
    qpos = lax.broadcasted_iota(jnp.int32, (TK, TQ), 1) + i * TQ
    r = lax.broadcasted_iota(jnp.int32, (TK, TK), 0)
    c = lax.broadcasted_iota(jnp.int32, (TK, TK), 1)
    suffix = (r <= c).astype(bf16)

    def step(j, carry, diag):
        kb = _key_block(k_ref, j)
        out = []
        for hh in range(HEADS_PER_STEP):
            tail, acc = carry[hh]
            z = lax.dot_general(kb, qhs[hh], _NT, preferred_element_type=f32)
            sp = jnp.log(1.0 + jnp.exp(-jnp.abs(z)))
            log_beta = jnp.minimum(z, 0.0) - sp
            log_1m = -jnp.maximum(z, 0.0) - sp
            if diag:
                valid = kpos + j * TK < qpos
                log_1m = jnp.where(valid, log_1m, 0.0)
            hi = log_1m.astype(bf16)
            lo = (log_1m - hi.astype(f32)).astype(bf16)
            incl = (jnp.dot(suffix, hi, preferred_element_type=f32)
                    + jnp.dot(suffix, lo, preferred_element_type=f32))
            a = jnp.exp(log_beta + ((incl - log_1m) + tail))
            if diag:
                a = jnp.where(valid, a, 0.0)
            vt = vt_ref[0, j, hh * HEAD_DIM:(hh + 1) * HEAD_DIM, :]
            acc = acc + jnp.dot(vt, a.astype(bf16), preferred_element_type=f32)
            out.append((tail + incl[0:1, :], acc))
        return tuple(out)

    carry = tuple((jnp.zeros((1, TQ), f32), jnp.zeros((HEAD_DIM, TQ), f32)) for _ in range(HEADS_PER_STEP))
    n_full = i * (TQ // TK)
    for d in reversed(range(TQ // TK)):
        carry = step(n_full + d, carry, diag=True)
    carry = lax.fori_loop(0, n_full, lambda t, cr: step(n_full - 1 - t, cr, diag=False), carry)
    o_ref[0] = jnp.concatenate([acc.T for _, acc in carry], axis=1).astype(bf16)


def _sb_attn(q, k, vt):
    B, S, D = q.shape
    n_hp = N_HEADS // HEADS_PER_STEP
    return pl.pallas_call(
        _sb_attn_kernel,
        grid=(B, n_hp, S // TQ),
        in_specs=[
            pl.BlockSpec((1, TQ, LANES), lambda b, h, i: (b, i, h)),
            pl.BlockSpec((1, S, LANES), lambda b, h, i: (b, 0, h)),
            pl.BlockSpec((1, S // TK, LANES, TK), lambda b, h, i: (b, 0, h, 0)),
        ],
        out_specs=pl.BlockSpec((1, TQ, LANES), lambda b, h, i: (b, i, h)),
        out_shape=jax.ShapeDtypeStruct((B, S, D), bf16),
        compiler_params=_params("arbitrary", "arbitrary", "arbitrary"),
        name="stickbreak_attention",
    )(q, k, vt)


def _glu_kernel(x_ref, mod_ref, w_ref, b_ref, o_ref):
    h = _modulated(x_ref, mod_ref)
    ag = jnp.dot(h, w_ref[...], preferred_element_type=f32) + b_ref[...]
    o_ref[0] = ag[:, :D_MODEL] * _sigmoid(ag[:, D_MODEL:])


def _glu(x, mod_l, w_in, b_in):
    B, S, D = x.shape
    row = lambda b, i: (b, i, 0)
    return pl.pallas_call(
        _glu_kernel,
        grid=(B, S // TM),
        in_specs=[
            pl.BlockSpec((1, TM, D), row),
            pl.BlockSpec((1, 1, 6, D), lambda b, i: (b, 0, 0, 0)),
            _const_spec((D, 2 * D)),
            _const_spec((1, 2 * D)),
        ],
        out_specs=pl.BlockSpec((1, TM, D), row),
        out_shape=jax.ShapeDtypeStruct((B, S, D), f32),
        compiler_params=_params("arbitrary", "arbitrary"),
        name="conv_glu",
    )(x, mod_l.reshape(B, 1, 6, D), w_in, b_in)


def _dwconv_kernel(halo_ref, y_ref, dw_ref, dwb_ref, lng_ref, lnb_ref, o_ref, ybuf_ref):
    i = pl.program_id(1)
    halo = halo_ref[0]
    ybuf_ref[0:CONV_HALO, :] = jnp.where(i > 0, halo, jnp.zeros_like(halo))
    ybuf_ref[CONV_HALO:, :] = y_ref[0]
    first = CONV_HALO - (CONV_WIDTH - 1)
    for cb in range(D_MODEL // LANES):
        cols = slice(cb * LANES, (cb + 1) * LANES)
        acc = jnp.zeros((TM, LANES), f32)
        for t in range(CONV_WIDTH):
            acc = acc + ybuf_ref[first + t:first + t + TM, cols] * dw_ref[t:t + 1, cols]
        ybuf_ref[CONV_HALO:, cols] = acc + dwb_ref[:, cols]
    yn = _ln(ybuf_ref[CONV_HALO:, :], lng_ref[...], lnb_ref[...])
    o_ref[0] = (yn * _sigmoid(yn)).astype(bf16)


def _dwconv(y, dw, dw_b, ln_g, ln_b):
    B, S, D = y.shape
    per = TM // CONV_HALO
    row = lambda b, i: (b, i, 0)
    return pl.pallas_call(
        _dwconv_kernel,
        grid=(B, S // TM),
        in_specs=[
            pl.BlockSpec((1, CONV_HALO, D), lambda b, i: (b, jnp.maximum(i * per - 1, 0), 0)),
            pl.BlockSpec((1, TM, D), row),
            _const_spec((CONV_WIDTH, D)),
            _const_spec((1, D)),
            _const_spec((1, D)),
            _const_spec((1, D)),
        ],
        out_specs=pl.BlockSpec((1, TM, D), row),
        out_shape=jax.ShapeDtypeStruct((B, S, D), bf16),
        scratch_shapes=[pltpu.VMEM((CONV_HALO + TM, D), f32)],
        compiler_params=_params("arbitrary", "arbitrary"),
        name="conv_dw_ln_swish",
    )(y, y, dw, dw_b, ln_g, ln_b)


def kernel(x, c, mod_w, mod_b, ln1_g, ln1_b, ln2_g, ln2_b, ffn_w_in, ffn_w_out, gm_w_in, gm_b_in, gm_ln_g, gm_ln_b, gm_w_s, gm_b_s, gm_w_out, fox_w_in, fox_b_f, fox_w_out, sb_w_in, sb_w_out, cv_w_in, cv_b_in, cv_dw, cv_dw_b, cv_ln_g, cv_ln_b, cv_w_out, cv_b_out):
    B, S, D = x.shape
    assert (B, S, D) == (16, 4096, D_MODEL) and S % TM == 0 and TQ % TK == 0
    row = lambda v: v.reshape(1, -1)
    mod = _modulation(c, mod_w, mod_b)
    zero_bias = jnp.zeros((1, D), f32)
    for l in range(DEPTH):
        m, j = l % 4, l // 4
        if m == 0:
            bs_full = jnp.repeat(gm_b_s[j].T, GM_GROUP_DIM, axis=1)
            ypre = _gmlp_pre(x, mod[l], gm_w_in[j].astype(bf16), row(gm_b_in[j]), row(gm_ln_g[j]),
                             row(gm_ln_b[j]), gm_w_s[j], bs_full)
            w_mo, b_mo = gm_w_out[j], zero_bias
        elif m == 1:
            w = fox_w_in[j]
            w_f = jnp.pad(w[:, 3 * D:], ((0, 0), (0, LANES - N_HEADS))).astype(bf16)
            q, k, vt, lf = _attn_in(x, mod[l], w[:, :3 * D].astype(bf16), w_f, fox_b_f[j].reshape(N_HEADS, 1))
            ypre = _fox_attn(q, k, vt, _cumsum(lf))
            w_mo, b_mo = fox_w_out[j], zero_bias
        elif m == 2:
            q, k, vt = _attn_in(x, mod[l], sb_w_in[j].astype(bf16))
            ypre = _sb_attn(q, k, vt)
            w_mo, b_mo = sb_w_out[j], zero_bias
        else:
            y = _glu(x, mod[l], cv_w_in[j].astype(bf16), row(cv_b_in[j]))
            ypre = _dwconv(y, cv_dw[j], row(cv_dw_b[j]), row(cv_ln_g[j]), row(cv_ln_b[j]))
            w_mo, b_mo = cv_w_out[j], row(cv_b_out[j])
        x = _post(x, ypre, mod[l], w_mo.astype(bf16), b_mo, row(ln1_g[l]), row(ln1_b[l]),
                  ffn_w_in[l].astype(bf16), ffn_w_out[l].astype(bf16), row(ln2_g[l]), row(ln2_b[l]))
    return x
```

```python
import functools

import jax
import jax.numpy as jnp
from jax import lax
from jax.experimental import pallas as pl
from jax.experimental.pallas import tpu as pltpu

f32 = jnp.float32
bf16 = jnp.bfloat16

D_MODEL = 1024
DEPTH = 4
HEAD_DIM = 64
N_HEADS = D_MODEL // HEAD_DIM
HEADS_PER_STEP = 2
LANES = 128
SUBLANES = 8
GM_CHUNK = 128
GM_GROUPS = 8
GM_GROUP_DIM = D_MODEL // GM_GROUPS
CONV_WIDTH = 31
CONV_HALO = 32
FFN_HIDDEN = 2816
DN_ALPHA = (2.0 * DEPTH) ** 0.25
LN_EPS = 1e-5
NEG_INF = -1e30
LOG2E = 1.4426950408889634
Q_SCALE = HEAD_DIM ** -0.5 * LOG2E
SB_UNDERFLOW = -151.0

TM = 256
TM_POST = 512
TQ = 512
TK = 256
VMEM_LIMIT = 56 * 1024 * 1024

_NT = (((1,), (1,)), ((), ()))


def _ln(x, g, b):
    mu = jnp.mean(x, axis=-1, keepdims=True)
    xc = x - mu
    var = jnp.mean(xc * xc, axis=-1, keepdims=True)
    return xc * lax.rsqrt(var + LN_EPS) * g + b


def _sigmoid(x):
    return 1.0 / (1.0 + jnp.exp(-x))


def _log_sigmoid(x):
    return jnp.minimum(x, 0.0) - jnp.log(1.0 + jnp.exp(-jnp.abs(x)))


def _gelu(x):
    return 0.5 * x * (1.0 + lax.erf(x * (2.0 ** -0.5)))


def _params(*sem):
    return pltpu.CompilerParams(dimension_semantics=sem, vmem_limit_bytes=VMEM_LIMIT)


def _const_spec(shape):
    nd = len(shape)
    return pl.BlockSpec(shape, lambda *_: (0,) * nd, pipeline_mode=pl.Buffered(1))


def _mod_kernel(c_ref, w_ref, b_ref, o_ref):
    c = c_ref[...]
    ca = (c * _sigmoid(c)).astype(bf16)
    o_ref[0, 0] = jnp.dot(ca, w_ref[0].astype(bf16), preferred_element_type=f32) + b_ref[0, 0]


def _modulation(c, mod_w, mod_b):
    B, D = c.shape
    L = mod_w.shape[0]
    out = pl.pallas_call(
        _mod_kernel,
        grid=(L, 6),
        in_specs=[
            pl.BlockSpec((B, D), lambda l, j: (0, 0)),
            pl.BlockSpec((1, D, D), lambda l, j: (l, 0, j)),
            pl.BlockSpec((1, 1, 1, D), lambda l, j: (l, j, 0, 0)),
        ],
        out_specs=pl.BlockSpec((1, 1, B, D), lambda l, j: (l, j, 0, 0)),
        out_shape=jax.ShapeDtypeStruct((L, 6, B, D), f32),
        compiler_params=_params("arbitrary", "arbitrary"),
        name="adaln_mod",
    )(c, mod_w, mod_b.reshape(L, 6, 1, D))
    return out.transpose(0, 2, 1, 3)


def _post_kernel(x_ref, y_ref, mod_ref, wmo_ref, bmo_ref, ln1g_ref, ln1b_ref, win_ref, wout_ref,
                 ln2g_ref, ln2b_ref, o_ref):
    mod = mod_ref[0, 0]
    g1, sh2, sc2, g2 = mod[2:3], mod[3:4], mod[4:5], mod[5:6]
    for r in range(TM_POST // TM):
        rows = slice(r * TM, (r + 1) * TM)
        x = x_ref[0, rows, :]
        y = jnp.dot(y_ref[0, rows, :], wmo_ref[...], preferred_element_type=f32) + bmo_ref[...]
        x1 = _ln(DN_ALPHA * x + (1.0 + g1) * y, ln1g_ref[...], ln1b_ref[...])
        h = (x1 * (1.0 + sc2) + sh2).astype(bf16)
        gu = jnp.dot(h, win_ref[...], preferred_element_type=f32)
        g = gu[:, :FFN_HIDDEN]
        u = gu[:, FFN_HIDDEN:]
        a = (g * _sigmoid(g) * u).astype(bf16)
        y2 = jnp.dot(a, wout_ref[...], preferred_element_type=f32)
        o_ref[0, rows, :] = _ln(DN_ALPHA * x1 + (1.0 + g2) * y2, ln2g_ref[...], ln2b_ref[...])


def _post(x, ypre, mod_l, w_mix_out, b_mix_out, ln1g, ln1b, w_in, w_out, ln2g, ln2b):
    B, S, D = x.shape
    row = lambda b, i: (b, i, 0)
    return pl.pallas_call(
        _post_kernel,
        grid=(B, S // TM_POST),
        in_specs=[
            pl.BlockSpec((1, TM_POST, D), row),
            pl.BlockSpec((1, TM_POST, D), row),
            pl.BlockSpec((1, 1, 6, D), lambda b, i: (b, 0, 0, 0)),
            _const_spec((D, D)),
            _const_spec((1, D)),
            _const_spec((1, D)),
            _const_spec((1, D)),
            _const_spec((D, 2 * FFN_HIDDEN)),
            _const_spec((FFN_HIDDEN, D)),
            _const_spec((1, D)),
            _const_spec((1, D)),
        ],
        out_specs=pl.BlockSpec((1, TM_POST, D), row),
        out_shape=jax.ShapeDtypeStruct((B, S, D), f32),
        compiler_params=_params("arbitrary", "arbitrary"),
        name="mixout_ln_ffn_ln",
    )(x, ypre, mod_l.reshape(B, 1, 6, D), w_mix_out, b_mix_out, ln1g, ln1b, w_in, w_out, ln2g, ln2b)


def _gmlp_kernel(x_ref, mod_ref, win_ref, bin_ref, lng_ref, lnb_ref, ws_ref, bs_ref, o_ref):
    h = _modulated(x_ref, mod_ref)
    z = _gelu(jnp.dot(h, win_ref[...], preferred_element_type=f32) + bin_ref[...])
    u = z[:, :D_MODEL]
    v = _ln(z[:, D_MODEL:], lng_ref[...], lnb_ref[...]).astype(bf16)
    r = lax.broadcasted_iota(jnp.int32, (GM_CHUNK, GM_CHUNK), 0)
    c = lax.broadcasted_iota(jnp.int32, (GM_CHUNK, GM_CHUNK), 1)
    for g in range(GM_GROUPS):
        cols = slice(g * GM_GROUP_DIM, (g + 1) * GM_GROUP_DIM)
        wm = jnp.where(r >= c, ws_ref[g], 0.0).astype(bf16)
        for n in range(TM // GM_CHUNK):
            rows = slice(n * GM_CHUNK, (n + 1) * GM_CHUNK)
            sv = jnp.dot(wm, v[rows, cols], preferred_element_type=f32) + bs_ref[:, cols]
            o_ref[0, rows, cols] = (u[rows, cols] * sv).astype(bf16)


def _gmlp_pre(x, mod_l, w_in, b_in, ln_g, ln_b, w_s, bs_full):
    B, S, D = x.shape
    row = lambda b, i: (b, i, 0)
    return pl.pallas_call(
        _gmlp_kernel,
        grid=(B, S // TM),
        in_specs=[
            pl.BlockSpec((1, TM, D), row),
            pl.BlockSpec((1, 1, 6, D), lambda b, i: (b, 0, 0, 0)),
            _const_spec((D, 2 * D)),
            _const_spec((1, 2 * D)),
            _const_spec((1, D)),
            _const_spec((1, D)),
            _const_spec((GM_GROUPS, GM_CHUNK, GM_CHUNK)),
            _const_spec((GM_CHUNK, D)),
        ],
        out_specs=pl.BlockSpec((1, TM, D), row),
        out_shape=jax.ShapeDtypeStruct((B, S, D), bf16),
        compiler_params=_params("arbitrary", "arbitrary"),
        name="gmlp_pre",
    )(x, mod_l.reshape(B, 1, 6, D), w_in, b_in, ln_g, ln_b, w_s, bs_full)


def _modulated(x_ref, mod_ref):
    mod = mod_ref[0, 0]
    return (x_ref[0] * (1.0 + mod[1:2]) + mod[0:1]).astype(bf16)


def _qkv_store(h, w_ref, q_ref, k_ref, vt_ref):
    qkv = jnp.dot(h, w_ref[...], preferred_element_type=f32)
    q_ref[0] = (qkv[:, :D_MODEL] * Q_SCALE).astype(bf16)
    k_ref[0] = qkv[:, D_MODEL:2 * D_MODEL].astype(bf16)
    vt_ref[0, 0] = qkv[:, 2 * D_MODEL:].T.astype(bf16)


def _qkv_kernel(x_ref, mod_ref, w_ref, q_ref, k_ref, vt_ref):
    _qkv_store(_modulated(x_ref, mod_ref), w_ref, q_ref, k_ref, vt_ref)


def _qkvf_kernel(x_ref, mod_ref, w_ref, wf_ref, bf_ref, q_ref, k_ref, vt_ref, lf_ref):
    h = _modulated(x_ref, mod_ref)
    _qkv_store(h, w_ref, q_ref, k_ref, vt_ref)
    f = jnp.dot(h, wf_ref[...], preferred_element_type=f32)
    ft = f.T[:N_HEADS] + bf_ref[...]
    lf_ref[0] = _log_sigmoid(ft) * LOG2E


def _attn_in(x, mod_l, w_qkv, w_f=None, b_f=None):
    B, S, D = x.shape
    row = lambda b, i: (b, i, 0)
    in_specs = [
        pl.BlockSpec((1, TK, D), row),
        pl.BlockSpec((1, 1, 6, D), lambda b, i: (b, 0, 0, 0)),
        _const_spec((D, 3 * D)),
    ]
    out_specs = [
        pl.BlockSpec((1, TK, D), row),
        pl.BlockSpec((1, TK, D), row),
        pl.BlockSpec((1, 1, D, TK), lambda b, i: (b, i, 0, 0)),
    ]
    out_shape = [
        jax.ShapeDtypeStruct((B, S, D), bf16),
        jax.ShapeDtypeStruct((B, S, D), bf16),
        jax.ShapeDtypeStruct((B, S // TK, D, TK), bf16),
    ]
    args = [x, mod_l.reshape(B, 1, 6, D), w_qkv]
    if w_f is None:
        body, name = _qkv_kernel, "attn_qkv"
    else:
        body, name = _qkvf_kernel, "attn_qkv_forget"
        in_specs += [_const_spec((D, LANES)), _const_spec((N_HEADS, 1))]
        out_specs += [pl.BlockSpec((1, N_HEADS, TK), lambda b, i: (b, 0, i))]
        out_shape += [jax.ShapeDtypeStruct((B, N_HEADS, S), f32)]
        args += [w_f, b_f]
    return pl.pallas_call(
        body,
        grid=(B, S // TK),
        in_specs=in_specs,
        out_specs=out_specs,
        out_shape=out_shape,
        compiler_params=_params("arbitrary", "arbitrary"),
        name=name,
    )(*args)


def _cumsum_kernel(lf_ref, f_ref):
    S = lf_ref.shape[2]
    r = lax.broadcasted_iota(jnp.int32, (LANES, LANES), 0)
    c = lax.broadcasted_iota(jnp.int32, (LANES, LANES), 1)
    upper = (r <= c).astype(f32)
    carry = jnp.zeros((N_HEADS, 1), f32)
    for j in range(S // LANES):
        cols = slice(j * LANES, (j + 1) * LANES)
        cs = jnp.dot(lf_ref[0, :, cols], upper, preferred_element_type=f32,
                     precision=lax.Precision.HIGHEST) + carry
        f_ref[0, :, cols] = cs
        carry = cs[:, LANES - 1:LANES]


def _cumsum(lf):
    B, H, S = lf.shape
    return pl.pallas_call(
        _cumsum_kernel,
        grid=(B,),
        in_specs=[pl.BlockSpec((1, H, S), lambda b: (b, 0, 0))],
        out_specs=pl.BlockSpec((1, H, S), lambda b: (b, 0, 0)),
        out_shape=jax.ShapeDtypeStruct((B, H, S), f32),
        compiler_params=_params("arbitrary"),
        name="forget_cumsum",
    )(lf)


def _head_mask(q2, hh):
    lane = lax.broadcasted_iota(jnp.int32, q2.shape, 1)
    return jnp.where((lane // HEAD_DIM) == hh, q2, jnp.zeros_like(q2))


def _key_block(k_ref, j):
    return k_ref[0, pl.ds(pl.multiple_of(j * TK, TK), TK), :]


def _fox_attn_kernel(q_ref, k_ref, vt_ref, fq_ref, fk_ref, o_ref, fcol_ref, s_ref, p_ref, acc_ref):
    i = pl.program_id(2)
    n_kb = fk_ref.shape[3] // TK

    @pl.when(i == 0)
    def _():
        for hh in range(HEADS_PER_STEP):
            for j in range(n_kb):
                rowv = fk_ref[0, 0, hh:hh + 1, j * TK:(j + 1) * TK]
                fcol_ref[hh, j * TK:(j + 1) * TK, :] = jnp.broadcast_to(rowv, (LANES, TK)).T

    q2 = q_ref[0]
    qhs = [_head_mask(q2, hh) for hh in range(HEADS_PER_STEP)]
    frows = [fq_ref[0, 0, hh:hh + 1, :] for hh in range(HEADS_PER_STEP)]
    rel = (lax.broadcasted_iota(jnp.int32, (TK, TQ), 0) - lax.broadcasted_iota(jnp.int32, (TK, TQ), 1))
    n_full = i * (TQ // TK)
    for hh in range(HEADS_PER_STEP):
        acc_ref[hh] = jnp.zeros((HEAD_DIM, TQ), f32)

    def scores(j, slot):
        kb = _key_block(k_ref, j)
        for hh in range(HEADS_PER_STEP):
            s_ref[slot, hh] = lax.dot_general(kb, qhs[hh], _NT, preferred_element_type=f32)

    def softmax(j, slot, stats, masked):
        out = []
        for hh in range(HEADS_PER_STEP):
            m, l, _ = stats[hh]
            fk = fcol_ref[hh, pl.ds(pl.multiple_of(j * TK, TK), TK), :]
            s = (s_ref[slot, hh] + frows[hh]) - jnp.concatenate([fk] * (TQ // LANES), axis=1)
            if masked:
                s = jnp.where(rel <= i * TQ - j * TK, s, NEG_INF)
            m_new = jnp.maximum(m, jnp.max(s, axis=0, keepdims=True))
            alpha = jnp.exp2(m - m_new)
            p = jnp.exp2(s - m_new)
            l = alpha * l + jnp.sum(p, axis=0, keepdims=True)
            p_ref[slot, hh] = p.astype(bf16)
            out.append((m_new, l, alpha))
        return tuple(out)

    def pv(j, slot, stats):
        for hh in range(HEADS_PER_STEP):
            vt = vt_ref[0, j, hh * HEAD_DIM:(hh + 1) * HEAD_DIM, :]
            acc_ref[hh] = stats[hh][2] * acc_ref[hh] + jnp.dot(vt, p_ref[slot, hh], preferred_element_type=f32)

    def finish(stats):
        o_ref[0] = jnp.concatenate([(acc_ref[hh] * (1.0 / stats[hh][1])).T for hh in range(HEADS_PER_STEP)],
                                   axis=1).astype(bf16)

    row = lambda v: jnp.full((1, TQ), v, f32)
    stats0 = tuple((row(NEG_INF), row(0.0), row(0.0)) for _ in range(HEADS_PER_STEP))
    d0, d1 = n_full, n_full + 1

    @pl.when(i == 0)
    def _():
        scores(d0, 0)
        scores(d1, 1)
        st = softmax(d0, 0, stats0, True)
        pv(d0, 0, st)
        st = softmax(d1, 1, st, True)
        pv(d1, 1, st)
        finish(st)

    @pl.when(i > 0)
    def _():
        scores(d0, 0)
        scores(d1, 1)
        st = softmax(d0, 0, stats0, True)
        pv(d0, 0, st)
        scores(0, 0)
        st = softmax(d1, 1, st, True)

        pv(d1, 1, st)
        scores(1, 1)
        st = softmax(0, 0, st, False)

        def two_ticks(u, st):
            pv(2 * u, 0, st)
            scores(2 * u + 2, 0)
            st = softmax(2 * u + 1, 1, st, False)
            pv(2 * u + 1, 1, st)
            scores(2 * u + 3, 1)
            return softmax(2 * u + 2, 0, st, False)

        st = lax.fori_loop(0, i - 1, two_ticks, st)
        pv(n_full - 2, 0, st)
        st = softmax(n_full - 1, 1, st, False)
        pv(n_full - 1, 1, st)
        finish(st)


def _fox_attn(q, k, vt, F):
    B, S, D = q.shape
    n_hp = N_HEADS // HEADS_PER_STEP
    F4 = F.reshape(B, n_hp, HEADS_PER_STEP, S)
    return pl.pallas_call(
        _fox_attn_kernel,
        grid=(B, n_hp, S // TQ),
        in_specs=[
            pl.BlockSpec((1, TQ, LANES), lambda b, h, i: (b, i, h)),
            pl.BlockSpec((1, S, LANES), lambda b, h, i: (b, 0, h)),
            pl.BlockSpec((1, S // TK, LANES, TK), lambda b, h, i: (b, 0, h, 0)),
            pl.BlockSpec((1, 1, HEADS_PER_STEP, TQ), lambda b, h, i: (b, h, 0, i)),
            pl.BlockSpec((1, 1, HEADS_PER_STEP, S), lambda b, h, i: (b, h, 0, 0)),
        ],
        out_specs=pl.BlockSpec((1, TQ, LANES), lambda b, h, i: (b, i, h)),
        out_shape=jax.ShapeDtypeStruct((B, S, D), bf16),
        scratch_shapes=[
            pltpu.VMEM((HEADS_PER_STEP, S, LANES), f32),
            pltpu.VMEM((2, HEADS_PER_STEP, TK, TQ), f32),
            pltpu.VMEM((2, HEADS_PER_STEP, TK, TQ), bf16),
            pltpu.VMEM((HEADS_PER_STEP, HEAD_DIM, TQ), f32),
        ],
        compiler_params=_params("arbitrary", "arbitrary", "arbitrary"),
        name="fox_attention",
    )(q, k, vt, F4, F4)


def _sb_attn_kernel(q_ref, k_ref, vt_ref, o_ref):
    i = pl.program_id(2)
    q2 = q_ref[0]
    qhs = [_head_mask(q2, hh) for hh in range(HEADS_PER_STEP)]
    rel = (lax.broadcasted_iota(jnp.int32, (TK, TQ), 0) - lax.broadcasted_iota(jnp.int32, (TK, TQ), 1))
    r = lax.broadcasted_iota(jnp.int32, (TK, TK), 0)
    c = lax.broadcasted_iota(jnp.int32, (TK, TK), 1)
    suffix = (r <= c).astype(bf16)

    def step(j, carry, diag):
        kb = _key_block(k_ref, j)
        out = []
        for hh in range(HEADS_PER_STEP):
            tail, acc = carry[hh]
            z = lax.dot_general(kb, qhs[hh], _NT, preferred_element_type=f32)
            log_1m = -(jnp.maximum(z, 0.0) + jnp.log2(1.0 + jnp.exp2(-jnp.abs(z))))
            if diag:
                valid = rel < i * TQ - j * TK
                log_1m = jnp.where(valid, log_1m, 0.0)
            incl = jnp.dot(suffix, log_1m.astype(bf16), preferred_element_type=f32)
            a = jnp.exp2((z + incl) + tail)
            if diag:
                a = jnp.where(valid, a, 0.0)
            vt = vt_ref[0, j, hh * HEAD_DIM:(hh + 1) * HEAD_DIM, :]
            acc = acc + jnp.dot(vt, a.astype(bf16), preferred_element_type=f32)
            out.append((tail + incl[0:1, :], acc))
        return tuple(out)

    def max_tail(carry):
        return functools.reduce(jnp.maximum, [jnp.max(tail) for tail, _ in carry])

    carry = tuple((jnp.zeros((1, TQ), f32), jnp.zeros((HEAD_DIM, TQ), f32)) for _ in range(HEADS_PER_STEP))
    n_full = i * (TQ // TK)
    for d in reversed(range(TQ // TK)):
        carry = step(n_full + d, carry, diag=True)

    def more(state):
        t, worst, _ = state
        return jnp.logical_and(t < n_full, worst >= SB_UNDERFLOW)

    def body(state):
        t, _, cr = state
        cr = step(n_full - 1 - t, cr, diag=False)
        return t + 1, max_tail(cr), cr

    _, _, carry = lax.while_loop(more, body, (jnp.int32(0), max_tail(carry), carry))
    o_ref[0] = jnp.concatenate([acc.T for _, acc in carry], axis=1).astype(bf16)


def _sb_attn(q, k, vt):
    B, S, D = q.shape
    n_hp = N_HEADS // HEADS_PER_STEP
    return pl.pallas_call(
        _sb_attn_kernel,
        grid=(B, n_hp, S // TQ),
        in_specs=[
            pl.BlockSpec((1, TQ, LANES), lambda b, h, i: (b, i, h)),
            pl.BlockSpec((1, S, LANES), lambda b, h, i: (b, 0, h)),
            pl.BlockSpec((1, S // TK, LANES, TK), lambda b, h, i: (b, 0, h, 0)),
        ],
        out_specs=pl.BlockSpec((1, TQ, LANES), lambda b, h, i: (b, i, h)),
        out_shape=jax.ShapeDtypeStruct((B, S, D), bf16),
        compiler_params=_params("arbitrary", "arbitrary", "arbitrary"),
        name="stickbreak_attention",
    )(q, k, vt)


def _glu_kernel(x_ref, mod_ref, w_ref, b_ref, o_ref):
    h = _modulated(x_ref, mod_ref)
    ag = jnp.dot(h, w_ref[...], preferred_element_type=f32) + b_ref[...]
    o_ref[0] = ag[:, :D_MODEL] * _sigmoid(ag[:, D_MODEL:])


def _glu(x, mod_l, w_in, b_in):
    B, S, D = x.shape
    row = lambda b, i: (b, i, 0)
    return pl.pallas_call(
        _glu_kernel,
        grid=(B, S // TM),
        in_specs=[
            pl.BlockSpec((1, TM, D), row),
            pl.BlockSpec((1, 1, 6, D), lambda b, i: (b, 0, 0, 0)),
            _const_spec((D, 2 * D)),
            _const_spec((1, 2 * D)),
        ],
        out_specs=pl.BlockSpec((1, TM, D), row),
        out_shape=jax.ShapeDtypeStruct((B, S, D), f32),
        compiler_params=_params("arbitrary", "arbitrary"),
        name="conv_glu",
    )(x, mod_l.reshape(B, 1, 6, D), w_in, b_in)


def _dwconv_kernel(halo_ref, y_ref, dw_ref, dwb_ref, lng_ref, lnb_ref, o_ref, ybuf_ref, z_ref):
    i = pl.program_id(1)
    halo = halo_ref[0]
    ybuf_ref[0:CONV_HALO, :] = jnp.where(i > 0, halo, jnp.zeros_like(halo))
    ybuf_ref[CONV_HALO:CONV_HALO + TM, :] = y_ref[0]
    ybuf_ref[CONV_HALO + TM:, :] = jnp.zeros((SUBLANES, D_MODEL), f32)
    first = CONV_HALO - (CONV_WIDTH - 1)
    for cb in range(D_MODEL // LANES):
        cols = slice(cb * LANES, (cb + 1) * LANES)
        acc = jnp.broadcast_to(dwb_ref[:, cols], (TM, LANES))
        for r in range(SUBLANES):
            z = None
            for a in range((CONV_HALO + SUBLANES) // SUBLANES):
                k = SUBLANES * a + r - first
                if 0 <= k < CONV_WIDTH:
                    term = ybuf_ref[SUBLANES * a:SUBLANES * a + TM + SUBLANES, cols] * dw_ref[k:k + 1, cols]
                    z = term if z is None else z + term
            if r == 0:
                acc = acc + z[:TM]
            else:
                z_ref[r] = z
                acc = acc + z_ref[r, r:r + TM, :]
        ybuf_ref[CONV_HALO:CONV_HALO + TM, cols] = acc
    yn = _ln(ybuf_ref[CONV_HALO:CONV_HALO + TM, :], lng_ref[...], lnb_ref[...])
    o_ref[0] = (yn * _sigmoid(yn)).astype(bf16)


def _dwconv(y, dw, dw_b, ln_g, ln_b):
    B, S, D = y.shape
    per = TM // CONV_HALO
    row = lambda b, i: (b, i, 0)
    return pl.pallas_call(
        _dwconv_kernel,
        grid=(B, S // TM),
        in_specs=[
            pl.BlockSpec((1, CONV_HALO, D), lambda b, i: (b, jnp.maximum(i * per - 1, 0), 0)),
            pl.BlockSpec((1, TM, D), row),
            _const_spec((CONV_WIDTH, D)),
            _const_spec((1, D)),
            _const_spec((1, D)),
            _const_spec((1, D)),
        ],
        out_specs=pl.BlockSpec((1, TM, D), row),
        out_shape=jax.ShapeDtypeStruct((B, S, D), bf16),
        scratch_shapes=[
            pltpu.VMEM((CONV_HALO + TM + SUBLANES, D), f32),
            pltpu.VMEM((SUBLANES, TM + SUBLANES, LANES), f32),
        ],
        compiler_params=_params("arbitrary", "arbitrary"),
        name="conv_dw_ln_swish",
    )(y, y, dw, dw_b, ln_g, ln_b)


def kernel(x, c, mod_w, mod_b, ln1_g, ln1_b, ln2_g, ln2_b, ffn_w_in, ffn_w_out, gm_w_in, gm_b_in, gm_ln_g, gm_ln_b, gm_w_s, gm_b_s, gm_w_out, fox_w_in, fox_b_f, fox_w_out, sb_w_in, sb_w_out, cv_w_in, cv_b_in, cv_dw, cv_dw_b, cv_ln_g, cv_ln_b, cv_w_out, cv_b_out):
    B, S, D = x.shape
    assert (B, S, D) == (16, 4096, D_MODEL) and S % TM_POST == 0 and TM_POST % TM == 0
    assert TQ == 2 * TK and TM == TK
    row = lambda v: v.reshape(1, -1)
    mod = _modulation(c, mod_w, mod_b)
    zero_bias = jnp.zeros((1, D), f32)
    for l in range(DEPTH):
        m, j = l % 4, l // 4
        if m == 0:
            bs_full = jnp.repeat(gm_b_s[j].T, GM_GROUP_DIM, axis=1)
            ypre = _gmlp_pre(x, mod[l], gm_w_in[j].astype(bf16), row(gm_b_in[j]), row(gm_ln_g[j]),
                             row(gm_ln_b[j]), gm_w_s[j], bs_full)
            w_mo, b_mo = gm_w_out[j], zero_bias
        elif m == 1:
            w = fox_w_in[j]
            w_f = jnp.pad(w[:, 3 * D:], ((0, 0), (0, LANES - N_HEADS))).astype(bf16)
            q, k, vt, lf = _attn_in(x, mod[l], w[:, :3 * D].astype(bf16), w_f, fox_b_f[j].reshape(N_HEADS, 1))
            ypre = _fox_attn(q, k, vt, _cumsum(lf))
            w_mo, b_mo = fox_w_out[j], zero_bias
        elif m == 2:
            q, k, vt = _attn_in(x, mod[l], sb_w_in[j].astype(bf16))
            ypre = _sb_attn(q, k, vt)
            w_mo, b_mo = sb_w_out[j], zero_bias
        else:
            y = _glu(x, mod[l], cv_w_in[j].astype(bf16), row(cv_b_in[j]))
            ypre = _dwconv(y, cv_dw[j], row(cv_dw_b[j]), row(cv_ln_g[j]), row(cv_ln_b[j]))
            w_mo, b_mo = cv_w_out[j], row(cv_b_out[j])
        x = _post(x, ypre, mod[l], w_mo.astype(bf16), b_mo, row(ln1_g[l]), row(ln1_b[l]),
                  ffn_w_in[l].astype(bf16), ffn_w_out[l].astype(bf16), row(ln2_g[l]), row(ln2_b[l]))
    return x
```

```python
import functools

import jax
import jax.numpy as jnp
from jax import lax
from jax.experimental import pallas as pl
from jax.experimental.pallas import tpu as pltpu

f32 = jnp.float32
bf16 = jnp.bfloat16

D_MODEL = 1024
DEPTH = 4
HEAD_DIM = 64
N_HEADS = D_MODEL // HEAD_DIM
HEADS_PER_STEP = 2
LANES = 128
SUBLANES = 8
ONES_ROWS = 16
GM_CHUNK = 128
GM_GROUPS = 8
GM_GROUP_DIM = D_MODEL // GM_GROUPS
CONV_WIDTH = 31
CONV_HALO = 32
FFN_HIDDEN = 2816
DN_ALPHA = (2.0 * DEPTH) ** 0.25
LN_EPS = 1e-5
NEG_INF = -1e30
LOG2E = 1.4426950408889634
Q_SCALE = HEAD_DIM ** -0.5 * LOG2E
SB_UNDERFLOW = -151.0
FOX_SKIP_MARGIN = 152.0

TM = 256
TM_POST = 512
TQ = 512
TK = 256
VMEM_LIMIT = 56 * 1024 * 1024

_NT = (((1,), (1,)), ((), ()))


def _ln(x, g, b):
    mu = jnp.mean(x, axis=-1, keepdims=True)
    xc = x - mu
    var = jnp.mean(xc * xc, axis=-1, keepdims=True)
    return xc * lax.rsqrt(var + LN_EPS) * g + b


def _sigmoid(x):
    return 1.0 / (1.0 + jnp.exp(-x))


def _log_sigmoid(x):
    return jnp.minimum(x, 0.0) - jnp.log(1.0 + jnp.exp(-jnp.abs(x)))


def _gelu(x):
    return 0.5 * x * (1.0 + lax.erf(x * (2.0 ** -0.5)))


def _params(*sem):
    return pltpu.CompilerParams(dimension_semantics=sem, vmem_limit_bytes=VMEM_LIMIT)


def _const_spec(shape):
    nd = len(shape)
    return pl.BlockSpec(shape, lambda *_: (0,) * nd, pipeline_mode=pl.Buffered(1))


def _mod_kernel(c_ref, w_ref, b_ref, o_ref):
    c = c_ref[...]
    ca = (c * _sigmoid(c)).astype(bf16)
    o_ref[0, 0] = jnp.dot(ca, w_ref[0].astype(bf16), preferred_element_type=f32) + b_ref[0, 0]


def _modulation(c, mod_w, mod_b):
    B, D = c.shape
    L = mod_w.shape[0]
    out = pl.pallas_call(
        _mod_kernel,
        grid=(L, 6),
        in_specs=[
            pl.BlockSpec((B, D), lambda l, j: (0, 0)),
            pl.BlockSpec((1, D, D), lambda l, j: (l, 0, j)),
            pl.BlockSpec((1, 1, 1, D), lambda l, j: (l, j, 0, 0)),
        ],
        out_specs=pl.BlockSpec((1, 1, B, D), lambda l, j: (l, j, 0, 0)),
        out_shape=jax.ShapeDtypeStruct((L, 6, B, D), f32),
        compiler_params=_params("arbitrary", "arbitrary"),
        name="adaln_mod",
    )(c, mod_w, mod_b.reshape(L, 6, 1, D))
    return out.transpose(0, 2, 1, 3)


def _post_kernel(x_ref, y_ref, mod_ref, wmo_ref, bmo_ref, ln1g_ref, ln1b_ref, win_ref, wout_ref,
                 ln2g_ref, ln2b_ref, o_ref):
    mod = mod_ref[0, 0]
    g1, sh2, sc2, g2 = mod[2:3], mod[3:4], mod[4:5], mod[5:6]
    for r in range(TM_POST // TM):
        rows = slice(r * TM, (r + 1) * TM)
        x = x_ref[0, rows, :]
        y = jnp.dot(y_ref[0, rows, :], wmo_ref[...], preferred_element_type=f32) + bmo_ref[...]
        x1 = _ln(DN_ALPHA * x + (1.0 + g1) * y, ln1g_ref[...], ln1b_ref[...])
        h = (x1 * (1.0 + sc2) + sh2).astype(bf16)
        gu = jnp.dot(h, win_ref[...], preferred_element_type=f32)
        g = gu[:, :FFN_HIDDEN]
        u = gu[:, FFN_HIDDEN:]
        a = (g * _sigmoid(g) * u).astype(bf16)
        y2 = jnp.dot(a, wout_ref[...], preferred_element_type=f32)
        o_ref[0, rows, :] = _ln(DN_ALPHA * x1 + (1.0 + g2) * y2, ln2g_ref[...], ln2b_ref[...])


def _post(x, ypre, mod_l, w_mix_out, b_mix_out, ln1g, ln1b, w_in, w_out, ln2g, ln2b):
    B, S, D = x.shape
    row = lambda b, i: (b, i, 0)
    return pl.pallas_call(
        _post_kernel,
        grid=(B, S // TM_POST),
        in_specs=[
            pl.BlockSpec((1, TM_POST, D), row),
            pl.BlockSpec((1, TM_POST, D), row),
            pl.BlockSpec((1, 1, 6, D), lambda b, i: (b, 0, 0, 0)),
            _const_spec((D, D)),
            _const_spec((1, D)),
            _const_spec((1, D)),
            _const_spec((1, D)),
            _const_spec((D, 2 * FFN_HIDDEN)),
            _const_spec((FFN_HIDDEN, D)),
            _const_spec((1, D)),
            _const_spec((1, D)),
        ],
        out_specs=pl.BlockSpec((1, TM_POST, D), row),
        out_shape=jax.ShapeDtypeStruct((B, S, D), f32),
        compiler_params=_params("arbitrary", "arbitrary"),
        name="mixout_ln_ffn_ln",
    )(x, ypre, mod_l.reshape(B, 1, 6, D), w_mix_out, b_mix_out, ln1g, ln1b, w_in, w_out, ln2g, ln2b)


def _gmlp_kernel(x_ref, mod_ref, win_ref, bin_ref, lng_ref, lnb_ref, ws_ref, bs_ref, o_ref):
    h = _modulated(x_ref, mod_ref)
    z = _gelu(jnp.dot(h, win_ref[...], preferred_element_type=f32) + bin_ref[...])
    u = z[:, :D_MODEL]
    v = _ln(z[:, D_MODEL:], lng_ref[...], lnb_ref[...]).astype(bf16)
    r = lax.broadcasted_iota(jnp.int32, (GM_CHUNK, GM_CHUNK), 0)
    c = lax.broadcasted_iota(jnp.int32, (GM_CHUNK, GM_CHUNK), 1)
    for g in range(GM_GROUPS):
        cols = slice(g * GM_GROUP_DIM, (g + 1) * GM_GROUP_DIM)
        wm = jnp.where(r >= c, ws_ref[g], 0.0).astype(bf16)
        for n in range(TM // GM_CHUNK):
            rows = slice(n * GM_CHUNK, (n + 1) * GM_CHUNK)
            sv = jnp.dot(wm, v[rows, cols], preferred_element_type=f32) + bs_ref[:, cols]
            o_ref[0, rows, cols] = (u[rows, cols] * sv).astype(bf16)


def _gmlp_pre(x, mod_l, w_in, b_in, ln_g, ln_b, w_s, bs_full):
    B, S, D = x.shape
    row = lambda b, i: (b, i, 0)
    return pl.pallas_call(
        _gmlp_kernel,
        grid=(B, S // TM),
        in_specs=[
            pl.BlockSpec((1, TM, D), row),
            pl.BlockSpec((1, 1, 6, D), lambda b, i: (b, 0, 0, 0)),
            _const_spec((D, 2 * D)),
            _const_spec((1, 2 * D)),
            _const_spec((1, D)),
            _const_spec((1, D)),
            _const_spec((GM_GROUPS, GM_CHUNK, GM_CHUNK)),
            _const_spec((GM_CHUNK, D)),
        ],
        out_specs=pl.BlockSpec((1, TM, D), row),
        out_shape=jax.ShapeDtypeStruct((B, S, D), bf16),
        compiler_params=_params("arbitrary", "arbitrary"),
        name="gmlp_pre",
    )(x, mod_l.reshape(B, 1, 6, D), w_in, b_in, ln_g, ln_b, w_s, bs_full)


def _modulated(x_ref, mod_ref):
    mod = mod_ref[0, 0]
    return (x_ref[0] * (1.0 + mod[1:2]) + mod[0:1]).astype(bf16)


def _qkv_store(h, w_ref, q_ref, k_ref, vt_ref):
    qkv = jnp.dot(h, w_ref[...], preferred_element_type=f32)
    q_ref[0] = (qkv[:, :D_MODEL] * Q_SCALE).astype(bf16)
    k_ref[0] = qkv[:, D_MODEL:2 * D_MODEL].astype(bf16)
    vt_ref[0, 0] = qkv[:, 2 * D_MODEL:].T.astype(bf16)


def _qkv_kernel(x_ref, mod_ref, w_ref, q_ref, k_ref, vt_ref):
    _qkv_store(_modulated(x_ref, mod_ref), w_ref, q_ref, k_ref, vt_ref)


def _qkvf_kernel(x_ref, mod_ref, w_ref, wf_ref, bf_ref, q_ref, k_ref, vt_ref, lf_ref):
    h = _modulated(x_ref, mod_ref)
    _qkv_store(h, w_ref, q_ref, k_ref, vt_ref)
    f = jnp.dot(h, wf_ref[...], preferred_element_type=f32)
    ft = f.T[:N_HEADS] + bf_ref[...]
    lf_ref[0] = _log_sigmoid(ft) * LOG2E


def _attn_in(x, mod_l, w_qkv, w_f=None, b_f=None):
    B, S, D = x.shape
    row = lambda b, i: (b, i, 0)
    in_specs = [
        pl.BlockSpec((1, TK, D), row),
        pl.BlockSpec((1, 1, 6, D), lambda b, i: (b, 0, 0, 0)),
        _const_spec((D, 3 * D)),
    ]
    out_specs = [
        pl.BlockSpec((1, TK, D), row),
        pl.BlockSpec((1, TK, D), row),
        pl.BlockSpec((1, 1, D, TK), lambda b, i: (b, i, 0, 0)),
    ]
    out_shape = [
        jax.ShapeDtypeStruct((B, S, D), bf16),
        jax.ShapeDtypeStruct((B, S, D), bf16),
        jax.ShapeDtypeStruct((B, S // TK, D, TK), bf16),
    ]
    args = [x, mod_l.reshape(B, 1, 6, D), w_qkv]
    if w_f is None:
        body, name = _qkv_kernel, "attn_qkv"
    else:
        body, name = _qkvf_kernel, "attn_qkv_forget"
        in_specs += [_const_spec((D, LANES)), _const_spec((N_HEADS, 1))]
        out_specs += [pl.BlockSpec((1, N_HEADS, TK), lambda b, i: (b, 0, i))]
        out_shape += [jax.ShapeDtypeStruct((B, N_HEADS, S), f32)]
        args += [w_f, b_f]
    return pl.pallas_call(
        body,
        grid=(B, S // TK),
        in_specs=in_specs,
        out_specs=out_specs,
        out_shape=out_shape,
        compiler_params=_params("arbitrary", "arbitrary"),
        name=name,
    )(*args)


def _cumsum_kernel(lf_ref, f_ref):
    S = lf_ref.shape[2]
    r = lax.broadcasted_iota(jnp.int32, (LANES, LANES), 0)
    c = lax.broadcasted_iota(jnp.int32, (LANES, LANES), 1)
    upper = (r <= c).astype(f32)
    carry = jnp.zeros((N_HEADS, 1), f32)
    for j in range(S // LANES):
        cols = slice(j * LANES, (j + 1) * LANES)
        cs = jnp.dot(lf_ref[0, :, cols], upper, preferred_element_type=f32,
                     precision=lax.Precision.HIGHEST) + carry
        f_ref[0, :, cols] = cs
        carry = cs[:, LANES - 1:LANES]


def _cumsum(lf):
    B, H, S = lf.shape
    return pl.pallas_call(
        _cumsum_kernel,
        grid=(B,),
        in_specs=[pl.BlockSpec((1, H, S), lambda b: (b, 0, 0))],
        out_specs=pl.BlockSpec((1, H, S), lambda b: (b, 0, 0)),
        out_shape=jax.ShapeDtypeStruct((B, H, S), f32),
        compiler_params=_params("arbitrary"),
        name="forget_cumsum",
    )(lf)


def _head_mask(q2, hh):
    lane = lax.broadcasted_iota(jnp.int32, q2.shape, 1)
    return jnp.where((lane // HEAD_DIM) == hh, q2, jnp.zeros_like(q2))


def _key_block(k_ref, j):
    return k_ref[0, pl.ds(pl.multiple_of(j * TK, TK), TK), :]


def _max_sq_norm(x2, hh):
    xf = _head_mask(x2, hh).astype(f32)
    return jnp.max(jnp.sum(xf * xf, axis=1, keepdims=True), axis=0, keepdims=True)


def _fox_attn_kernel(q_ref, k_ref, vt_ref, fq_ref, fk_ref, o_ref, fcol_ref, kn_ref, s_ref, p_ref, acc_ref):
    i = pl.program_id(2)
    n_kb = fk_ref.shape[3] // TK

    @pl.when(i == 0)
    def _():
        for hh in range(HEADS_PER_STEP):
            for j in range(n_kb):
                rowv = fk_ref[0, 0, hh:hh + 1, j * TK:(j + 1) * TK]
                fcol_ref[hh, j * TK:(j + 1) * TK, :] = jnp.broadcast_to(rowv, (LANES, TK)).T
            kn_ref[hh] = jnp.broadcast_to(_max_sq_norm(k_ref[0], hh), (SUBLANES, LANES))

    q2 = q_ref[0]
    qhs = [_head_mask(q2, hh) for hh in range(HEADS_PER_STEP)]
    frows = [fq_ref[0, 0, hh:hh + 1, :] for hh in range(HEADS_PER_STEP)]

    n_far = None
    for hh in range(HEADS_PER_STEP):
        bound = jnp.sqrt(_max_sq_norm(q2, hh) * kn_ref[hh, 0:1, 0:1])
        far = (fk_ref[0, 0, hh:hh + 1, :] - frows[hh][:, 0:1]) > 2.0 * bound + FOX_SKIP_MARGIN
        cnt = jnp.sum(far.astype(jnp.int32))
        n_far = cnt if n_far is None else jnp.minimum(n_far, cnt)
    rel = (lax.broadcasted_iota(jnp.int32, (TK, TQ), 0) - lax.broadcasted_iota(jnp.int32, (TK, TQ), 1))
    n_full = i * (TQ // TK)
    for hh in range(HEADS_PER_STEP):
        acc_ref[hh] = jnp.zeros((HEAD_DIM + ONES_ROWS, TQ), f32)
    ones = jnp.ones((ONES_ROWS, TK), bf16)

    def scores(j, slot):
        kb = _key_block(k_ref, j)
        for hh in range(HEADS_PER_STEP):
            s_ref[slot, hh] = lax.dot_general(kb, qhs[hh], _NT, preferred_element_type=f32)

    def softmax(j, slot, stats, masked):
        out = []
        for hh in range(HEADS_PER_STEP):
            m, _ = stats[hh]
            fk = fcol_ref[hh, pl.ds(pl.multiple_of(j * TK, TK), TK), :]
            s = (s_ref[slot, hh] + frows[hh]) - jnp.concatenate([fk] * (TQ // LANES), axis=1)
            if masked:
                s = jnp.where(rel <= i * TQ - j * TK, s, NEG_INF)
            m_new = jnp.maximum(m, jnp.max(s, axis=0, keepdims=True))
            p_ref[slot, hh] = jnp.exp2((s - m_new).astype(bf16))
            out.append((m_new, jnp.exp2(m - m_new)))
        return tuple(out)

    def pv(j, slot, stats):
        for hh in range(HEADS_PER_STEP):
            vt = vt_ref[0, j, hh * HEAD_DIM:(hh + 1) * HEAD_DIM, :]
            upd = jnp.dot(jnp.concatenate([vt, ones], axis=0), p_ref[slot, hh], preferred_element_type=f32)
            acc_ref[hh] = stats[hh][1] * acc_ref[hh] + upd

    def finish():
        outs = []
        for hh in range(HEADS_PER_STEP):
            acc = acc_ref[hh]
            outs.append((acc[:HEAD_DIM] * (1.0 / acc[HEAD_DIM:HEAD_DIM + 1])).T)
        o_ref[0] = jnp.concatenate(outs, axis=1).astype(bf16)

    row = lambda v: jnp.full((1, TQ), v, f32)
    stats0 = tuple((row(NEG_INF), row(0.0)) for _ in range(HEADS_PER_STEP))
    d0, d1 = n_full, n_full + 1

    @pl.when(i == 0)
    def _():
        scores(d0, 0)
        scores(d1, 1)
        st = softmax(d0, 0, stats0, True)
        pv(d0, 0, st)
        st = softmax(d1, 1, st, True)
        pv(d1, 1, st)
        finish()

    @pl.when(i > 0)
    def _():
        j0 = jnp.minimum((n_far // (2 * TK)) * 2, n_full - 2)
        scores(d0, 0)
        scores(d1, 1)
        st = softmax(d0, 0, stats0, True)
        pv(d0, 0, st)
        scores(j0, 0)
        st = softmax(d1, 1, st, True)

        pv(d1, 1, st)
        scores(j0 + 1, 1)
        st = softmax(j0, 0, st, False)

        def two_ticks(u, st):
            j = j0 + 2 * u
            pv(j, 0, st)
            scores(j + 2, 0)
            st = softmax(j + 1, 1, st, False)
            pv(j + 1, 1, st)
            scores(j + 3, 1)
            return softmax(j + 2, 0, st, False)

        st = lax.fori_loop(0, (n_full - j0) // 2 - 1, two_ticks, st)
        pv(n_full - 2, 0, st)
        st = softmax(n_full - 1, 1, st, False)
        pv(n_full - 1, 1, st)
        finish()


def _fox_attn(q, k, vt, F):
    B, S, D = q.shape
    n_hp = N_HEADS // HEADS_PER_STEP
    F4 = F.reshape(B, n_hp, HEADS_PER_STEP, S)
    return pl.pallas_call(
        _fox_attn_kernel,
        grid=(B, n_hp, S // TQ),
        in_specs=[
            pl.BlockSpec((1, TQ, LANES), lambda b, h, i: (b, i, h)),
            pl.BlockSpec((1, S, LANES), lambda b, h, i: (b, 0, h)),
            pl.BlockSpec((1, S // TK, LANES, TK), lambda b, h, i: (b, 0, h, 0)),
            pl.BlockSpec((1, 1, HEADS_PER_STEP, TQ), lambda b, h, i: (b, h, 0, i)),
            pl.BlockSpec((1, 1, HEADS_PER_STEP, S), lambda b, h, i: (b, h, 0, 0)),
        ],
        out_specs=pl.BlockSpec((1, TQ, LANES), lambda b, h, i: (b, i, h)),
        out_shape=jax.ShapeDtypeStruct((B, S, D), bf16),
        scratch_shapes=[
            pltpu.VMEM((HEADS_PER_STEP, S, LANES), f32),
            pltpu.VMEM((HEADS_PER_STEP, SUBLANES, LANES), f32),
            pltpu.VMEM((2, HEADS_PER_STEP, TK, TQ), f32),
            pltpu.VMEM((2, HEADS_PER_STEP, TK, TQ), bf16),
            pltpu.VMEM((HEADS_PER_STEP, HEAD_DIM + ONES_ROWS, TQ), f32),
        ],
        compiler_params=_params("arbitrary", "arbitrary", "arbitrary"),
        name="fox_attention",
    )(q, k, vt, F4, F4)


def _sb_attn_kernel(q_ref, k_ref, vt_ref, o_ref):
    i = pl.program_id(2)
    q2 = q_ref[0]
    qhs = [_head_mask(q2, hh) for hh in range(HEADS_PER_STEP)]
    rel = (lax.broadcasted_iota(jnp.int32, (TK, TQ), 0) - lax.broadcasted_iota(jnp.int32, (TK, TQ), 1))
    r = lax.broadcasted_iota(jnp.int32, (TK, TK), 0)
    c = lax.broadcasted_iota(jnp.int32, (TK, TK), 1)
    suffix = (r <= c).astype(bf16)

    def step(j, carry, diag, q0=0):
        kb = _key_block(k_ref, j)
        out = []
        for hh in range(HEADS_PER_STEP):
            tail, acc = carry[hh]
            z = lax.dot_general(kb, qhs[hh][q0:], _NT, preferred_element_type=f32)
            log_1m = -(jnp.maximum(z, 0.0) + jnp.log2(1.0 + jnp.exp2(-jnp.abs(z))))
            if diag:
                valid = rel[:, q0:] < i * TQ - j * TK
                log_1m = jnp.where(valid, log_1m, 0.0)
            incl = jnp.dot(suffix, log_1m.astype(bf16), preferred_element_type=f32)
            a = jnp.exp2((z + incl) + tail[:, q0:])
            if diag:
                a = jnp.where(valid, a, 0.0)
            vt = vt_ref[0, j, hh * HEAD_DIM:(hh + 1) * HEAD_DIM, :]
            upd = jnp.dot(vt, a.astype(bf16), preferred_element_type=f32)
            new_tail, new_acc = tail[:, q0:] + incl[0:1, :], acc[:, q0:] + upd
            if q0:
                new_tail = jnp.concatenate([tail[:, :q0], new_tail], axis=1)
                new_acc = jnp.concatenate([acc[:, :q0], new_acc], axis=1)
            out.append((new_tail, new_acc))
        return tuple(out)

    def max_tail(carry):
        return functools.reduce(jnp.maximum, [jnp.max(tail) for tail, _ in carry])

    def finish(carry):
        o_ref[0] = jnp.concatenate([acc.T for _, acc in carry], axis=1).astype(bf16)

    init = tuple((jnp.zeros((1, TQ), f32), jnp.zeros((HEAD_DIM, TQ), f32)) for _ in range(HEADS_PER_STEP))
    n_full = i * (TQ // TK)

    def diagonal(carry):
        for d in reversed(range(TQ // TK)):
            carry = step(n_full + d, carry, diag=True, q0=d * TK)
        return carry

    @pl.when(i == 0)
    def _():
        finish(diagonal(init))

    @pl.when(i > 0)
    def _():
        carry = step(n_full - 1, diagonal(init), diag=False)

        def more(state):
            t, worst, _ = state
            return jnp.logical_and(t < n_full, worst >= SB_UNDERFLOW)

        def body(state):
            t, _, cr = state
            cr = step(n_full - 1 - t, cr, diag=False)
            return t + 1, max_tail(cr), cr

        _, _, carry = lax.while_loop(more, body, (jnp.int32(1), max_tail(carry), carry))
        finish(carry)


def _sb_attn(q, k, vt):
    B, S, D = q.shape
    n_hp = N_HEADS // HEADS_PER_STEP
    return pl.pallas_call(
        _sb_attn_kernel,
        grid=(B, n_hp, S // TQ),
        in_specs=[
            pl.BlockSpec((1, TQ, LANES), lambda b, h, i: (b, i, h)),
            pl.BlockSpec((1, S, LANES), lambda b, h, i: (b, 0, h)),
            pl.BlockSpec((1, S // TK, LANES, TK), lambda b, h, i: (b, 0, h, 0)),
        ],
        out_specs=pl.BlockSpec((1, TQ, LANES), lambda b, h, i: (b, i, h)),
        out_shape=jax.ShapeDtypeStruct((B, S, D), bf16),
        compiler_params=_params("arbitrary", "arbitrary", "arbitrary"),
        name="stickbreak_attention",
    )(q, k, vt)


def _glu_kernel(x_ref, mod_ref, w_ref, b_ref, o_ref):
    h = _modulated(x_ref, mod_ref)
    ag = jnp.dot(h, w_ref[...], preferred_element_type=f32) + b_ref[...]
    o_ref[0] = ag[:, :D_MODEL] * _sigmoid(ag[:, D_MODEL:])


def _glu(x, mod_l, w_in, b_in):
    B, S, D = x.shape
    row = lambda b, i: (b, i, 0)
    return pl.pallas_call(
        _glu_kernel,
        grid=(B, S // TM),
        in_specs=[
            pl.BlockSpec((1, TM, D), row),
            pl.BlockSpec((1, 1, 6, D), lambda b, i: (b, 0, 0, 0)),
            _const_spec((D, 2 * D)),
            _const_spec((1, 2 * D)),
        ],
        out_specs=pl.BlockSpec((1, TM, D), row),
        out_shape=jax.ShapeDtypeStruct((B, S, D), f32),
        compiler_params=_params("arbitrary", "arbitrary"),
        name="conv_glu",
    )(x, mod_l.reshape(B, 1, 6, D), w_in, b_in)


def _dwconv_kernel(halo_ref, y_ref, dw_ref, dwb_ref, lng_ref, lnb_ref, o_ref, ybuf_ref, z_ref):
    i = pl.program_id(1)
    halo = halo_ref[0]
    ybuf_ref[0:CONV_HALO, :] = jnp.where(i > 0, halo, jnp.zeros_like(halo))
    ybuf_ref[CONV_HALO:CONV_HALO + TM, :] = y_ref[0]
    ybuf_ref[CONV_HALO + TM:, :] = jnp.zeros((SUBLANES, D_MODEL), f32)
    first = CONV_HALO - (CONV_WIDTH - 1)
    for cb in range(D_MODEL // LANES):
        cols = slice(cb * LANES, (cb + 1) * LANES)
        acc = jnp.broadcast_to(dwb_ref[:, cols], (TM, LANES))
        for r in range(SUBLANES):
            z = None
            for a in range((CONV_HALO + SUBLANES) // SUBLANES):
                k = SUBLANES * a + r - first
                if 0 <= k < CONV_WIDTH:
                    term = ybuf_ref[SUBLANES * a:SUBLANES * a + TM + SUBLANES, cols] * dw_ref[k:k + 1, cols]
                    z = term if z is None else z + term
            if r == 0:
                acc = acc + z[:TM]
            else:
                z_ref[r] = z
                acc = acc + z_ref[r, r:r + TM, :]
        ybuf_ref[CONV_HALO:CONV_HALO + TM, cols] = acc
    yn = _ln(ybuf_ref[CONV_HALO:CONV_HALO + TM, :], lng_ref[...], lnb_ref[...])
    o_ref[0] = (yn * _sigmoid(yn)).astype(bf16)


def _dwconv(y, dw, dw_b, ln_g, ln_b):
    B, S, D = y.shape
    per = TM // CONV_HALO
    row = lambda b, i: (b, i, 0)
    return pl.pallas_call(
        _dwconv_kernel,
        grid=(B, S // TM),
        in_specs=[
            pl.BlockSpec((1, CONV_HALO, D), lambda b, i: (b, jnp.maximum(i * per - 1, 0), 0)),
            pl.BlockSpec((1, TM, D), row),
            _const_spec((CONV_WIDTH, D)),
            _const_spec((1, D)),
            _const_spec((1, D)),
            _const_spec((1, D)),
        ],
        out_specs=pl.BlockSpec((1, TM, D), row),
        out_shape=jax.ShapeDtypeStruct((B, S, D), bf16),
        scratch_shapes=[
            pltpu.VMEM((CONV_HALO + TM + SUBLANES, D), f32),
            pltpu.VMEM((SUBLANES, TM + SUBLANES, LANES), f32),
        ],
        compiler_params=_params("arbitrary", "arbitrary"),
        name="conv_dw_ln_swish",
    )(y, y, dw, dw_b, ln_g, ln_b)


def kernel(x, c, mod_w, mod_b, ln1_g, ln1_b, ln2_g, ln2_b, ffn_w_in, ffn_w_out, gm_w_in, gm_b_in, gm_ln_g, gm_ln_b, gm_w_s, gm_b_s, gm_w_out, fox_w_in, fox_b_f, fox_w_out, sb_w_in, sb_w_out, cv_w_in, cv_b_in, cv_dw, cv_dw_b, cv_ln_g, cv_ln_b, cv_w_out, cv_b_out):
    B, S, D = x.shape
    assert (B, S, D) == (16, 4096, D_MODEL) and S % TM_POST == 0 and TM_POST % TM == 0
    assert TQ == 2 * TK and TM == TK
    row = lambda v: v.reshape(1, -1)
    mod = _modulation(c, mod_w, mod_b)
    zero_bias = jnp.zeros((1, D), f32)
    for l in range(DEPTH):
        m, j = l % 4, l // 4
        if m == 0:
            bs_full = jnp.repeat(gm_b_s[j].T, GM_GROUP_DIM, axis=1)
            ypre = _gmlp_pre(x, mod[l], gm_w_in[j].astype(bf16), row(gm_b_in[j]), row(gm_ln_g[j]),
                             row(gm_ln_b[j]), gm_w_s[j], bs_full)
            w_mo, b_mo = gm_w_out[j], zero_bias
        elif m == 1:
            w = fox_w_in[j]
            w_f = jnp.pad(w[:, 3 * D:], ((0, 0), (0, LANES - N_HEADS))).astype(bf16)
            q, k, vt, lf = _attn_in(x, mod[l], w[:, :3 * D].astype(bf16), w_f, fox_b_f[j].reshape(N_HEADS, 1))
            ypre = _fox_attn(q, k, vt, _cumsum(lf))
            w_mo, b_mo = fox_w_out[j], zero_bias
        elif m == 2:
            q, k, vt = _attn_in(x, mod[l], sb_w_in[j].astype(bf16))
            ypre = _sb_attn(q, k, vt)
            w_mo, b_mo = sb_w_out[j], zero_bias
        else:
            y = _glu(x, mod[l], cv_w_in[j].astype(bf16), row(cv_b_in[j]))
            ypre = _dwconv(y, cv_dw[j], row(cv_dw_b[j]), row(cv_ln_g[j]), row(cv_ln_b[j]))
            w_mo, b_mo = cv_w_out[j], row(cv_b_out[j])
        x = _post(x, ypre, mod[l], w_mo.astype(bf16), b_mo, row(ln1_g[l]), row(ln1_b[l]),
                  ffn_w_in[l].astype(bf16), ffn_w_out[l].astype(bf16), row(ln2_g[l]), row(ln2_b[l]))
    return x
```

```python
import functools

import jax
import jax.numpy as jnp
from jax import lax
from jax.experimental import pallas as pl
from jax.experimental.pallas import tpu as pltpu

f32 = jnp.float32
bf16 = jnp.bfloat16

D_MODEL = 1024
DEPTH = 4
HEAD_DIM = 64
N_HEADS = D_MODEL // HEAD_DIM
HEADS_PER_STEP = 2
SB_HEADS = 4
LANES = 128
SUBLANES = 8
ONES_ROWS = 16
GM_CHUNK = 128
GM_GROUPS = 8
GM_GROUP_DIM = D_MODEL // GM_GROUPS
CONV_WIDTH = 31
CONV_HALO = 32
FFN_HIDDEN = 2816
DN_ALPHA = (2.0 * DEPTH) ** 0.25
LN_EPS = 1e-5
NEG_INF = -1e30
LOG2E = 1.4426950408889634
Q_SCALE = HEAD_DIM ** -0.5 * LOG2E
SB_UNDERFLOW = -151.0
TM = 256
TM_POST = 512
TQ = 512
TK = 256
VMEM_LIMIT = 56 * 1024 * 1024

_NT = (((1,), (1,)), ((), ()))


def _ln(x, g, b):
    mu = jnp.mean(x, axis=-1, keepdims=True)
    xc = x - mu
    var = jnp.mean(xc * xc, axis=-1, keepdims=True)
    return xc * lax.rsqrt(var + LN_EPS) * g + b


def _sigmoid(x):
    return 1.0 / (1.0 + jnp.exp(-x))


def _log_sigmoid(x):
    return jnp.minimum(x, 0.0) - jnp.log(1.0 + jnp.exp(-jnp.abs(x)))


def _gelu(x):
    return 0.5 * x * (1.0 + lax.erf(x * (2.0 ** -0.5)))


def _params(*sem):
    return pltpu.CompilerParams(dimension_semantics=sem, vmem_limit_bytes=VMEM_LIMIT)


def _const_spec(shape):
    nd = len(shape)
    return pl.BlockSpec(shape, lambda *_: (0,) * nd, pipeline_mode=pl.Buffered(1))


def _mod_kernel(c_ref, w_ref, b_ref, o_ref):
    c = c_ref[...]
    ca = (c * _sigmoid(c)).astype(bf16)
    o_ref[0, 0] = jnp.dot(ca, w_ref[0].astype(bf16), preferred_element_type=f32) + b_ref[0, 0]


def _modulation(c, mod_w, mod_b):
    B, D = c.shape
    L = mod_w.shape[0]
    out = pl.pallas_call(
        _mod_kernel,
        grid=(L, 6),
        in_specs=[
            pl.BlockSpec((B, D), lambda l, j: (0, 0)),
            pl.BlockSpec((1, D, D), lambda l, j: (l, 0, j)),
            pl.BlockSpec((1, 1, 1, D), lambda l, j: (l, j, 0, 0)),
        ],
        out_specs=pl.BlockSpec((1, 1, B, D), lambda l, j: (l, j, 0, 0)),
        out_shape=jax.ShapeDtypeStruct((L, 6, B, D), f32),
        compiler_params=_params("arbitrary", "arbitrary"),
        name="adaln_mod",
    )(c, mod_w, mod_b.reshape(L, 6, 1, D))
    return out.transpose(0, 2, 1, 3)


def _post_kernel(x_ref, y_ref, mod_ref, wmo_ref, bmo_ref, ln1g_ref, ln1b_ref, win_ref, wout_ref,
                 ln2g_ref, ln2b_ref, o_ref):
    mod = mod_ref[0, 0]
    g1, sh2, sc2, g2 = mod[2:3], mod[3:4], mod[4:5], mod[5:6]
    for r in range(TM_POST // TM):
        rows = slice(r * TM, (r + 1) * TM)
        x = x_ref[0, rows, :]
        y = jnp.dot(y_ref[0, rows, :], wmo_ref[...], preferred_element_type=f32) + bmo_ref[...]
        x1 = _ln(DN_ALPHA * x + (1.0 + g1) * y, ln1g_ref[...], ln1b_ref[...])
        h = (x1 * (1.0 + sc2) + sh2).astype(bf16)
        gu = jnp.dot(h, win_ref[...], preferred_element_type=f32)
        g = gu[:, :FFN_HIDDEN]
        u = gu[:, FFN_HIDDEN:]
        a = (g * _sigmoid(g) * u).astype(bf16)
        y2 = jnp.dot(a, wout_ref[...], preferred_element_type=f32)
        o_ref[0, rows, :] = _ln(DN_ALPHA * x1 + (1.0 + g2) * y2, ln2g_ref[...], ln2b_ref[...])


def _post(x, ypre, mod_l, w_mix_out, b_mix_out, ln1g, ln1b, w_in, w_out, ln2g, ln2b):
    B, S, D = x.shape
    row = lambda b, i: (b, i, 0)
    return pl.pallas_call(
        _post_kernel,
        grid=(B, S // TM_POST),
        in_specs=[
            pl.BlockSpec((1, TM_POST, D), row),
            pl.BlockSpec((1, TM_POST, D), row),
            pl.BlockSpec((1, 1, 6, D), lambda b, i: (b, 0, 0, 0)),
            _const_spec((D, D)),
            _const_spec((1, D)),
            _const_spec((1, D)),
            _const_spec((1, D)),
            _const_spec((D, 2 * FFN_HIDDEN)),
            _const_spec((FFN_HIDDEN, D)),
            _const_spec((1, D)),
            _const_spec((1, D)),
        ],
        out_specs=pl.BlockSpec((1, TM_POST, D), row),
        out_shape=jax.ShapeDtypeStruct((B, S, D), f32),
        compiler_params=_params("arbitrary", "arbitrary"),
        name="mixout_ln_ffn_ln",
    )(x, ypre, mod_l.reshape(B, 1, 6, D), w_mix_out, b_mix_out, ln1g, ln1b, w_in, w_out, ln2g, ln2b)


def _gmlp_kernel(x_ref, mod_ref, win_ref, bin_ref, lng_ref, lnb_ref, ws_ref, bs_ref, o_ref):
    h = _modulated(x_ref, mod_ref)
    z = _gelu(jnp.dot(h, win_ref[...], preferred_element_type=f32) + bin_ref[...])
    u = z[:, :D_MODEL]
    v = _ln(z[:, D_MODEL:], lng_ref[...], lnb_ref[...]).astype(bf16)
    r = lax.broadcasted_iota(jnp.int32, (GM_CHUNK, GM_CHUNK), 0)
    c = lax.broadcasted_iota(jnp.int32, (GM_CHUNK, GM_CHUNK), 1)
    for g in range(GM_GROUPS):
        cols = slice(g * GM_GROUP_DIM, (g + 1) * GM_GROUP_DIM)
        wm = jnp.where(r >= c, ws_ref[g], 0.0).astype(bf16)
        for n in range(TM // GM_CHUNK):
            rows = slice(n * GM_CHUNK, (n + 1) * GM_CHUNK)
            sv = jnp.dot(wm, v[rows, cols], preferred_element_type=f32) + bs_ref[:, cols]
            o_ref[0, rows, cols] = (u[rows, cols] * sv).astype(bf16)


def _gmlp_pre(x, mod_l, w_in, b_in, ln_g, ln_b, w_s, bs_full):
    B, S, D = x.shape
    row = lambda b, i: (b, i, 0)
    return pl.pallas_call(
        _gmlp_kernel,
        grid=(B, S // TM),
        in_specs=[
            pl.BlockSpec((1, TM, D), row),
            pl.BlockSpec((1, 1, 6, D), lambda b, i: (b, 0, 0, 0)),
            _const_spec((D, 2 * D)),
            _const_spec((1, 2 * D)),
            _const_spec((1, D)),
            _const_spec((1, D)),
            _const_spec((GM_GROUPS, GM_CHUNK, GM_CHUNK)),
            _const_spec((GM_CHUNK, D)),
        ],
        out_specs=pl.BlockSpec((1, TM, D), row),
        out_shape=jax.ShapeDtypeStruct((B, S, D), bf16),
        compiler_params=_params("arbitrary", "arbitrary"),
        name="gmlp_pre",
    )(x, mod_l.reshape(B, 1, 6, D), w_in, b_in, ln_g, ln_b, w_s, bs_full)


def _modulated(x_ref, mod_ref):
    mod = mod_ref[0, 0]
    return (x_ref[0] * (1.0 + mod[1:2]) + mod[0:1]).astype(bf16)


def _qkv_store(h, w_ref, q_ref, k_ref, vt_ref):
    qkv = jnp.dot(h, w_ref[...], preferred_element_type=f32)
    q_ref[0] = (qkv[:, :D_MODEL] * Q_SCALE).astype(bf16)
    k_ref[0] = qkv[:, D_MODEL:2 * D_MODEL].astype(bf16)
    vt_ref[0, 0] = qkv[:, 2 * D_MODEL:].T.astype(bf16)


def _qkv_kernel(x_ref, mod_ref, w_ref, q_ref, k_ref, vt_ref):
    _qkv_store(_modulated(x_ref, mod_ref), w_ref, q_ref, k_ref, vt_ref)


def _qkvf_kernel(x_ref, mod_ref, w_ref, wf_ref, bf_ref, q_ref, k_ref, vt_ref, lf_ref):
    h = _modulated(x_ref, mod_ref)
    _qkv_store(h, w_ref, q_ref, k_ref, vt_ref)
    f = jnp.dot(h, wf_ref[...], preferred_element_type=f32)
    ft = f.T[:N_HEADS] + bf_ref[...]
    lf_ref[0] = _log_sigmoid(ft) * LOG2E


def _attn_in(x, mod_l, w_qkv, w_f=None, b_f=None):
    B, S, D = x.shape
    row = lambda b, i: (b, i, 0)
    in_specs = [
        pl.BlockSpec((1, TK, D), row),
        pl.BlockSpec((1, 1, 6, D), lambda b, i: (b, 0, 0, 0)),
        _const_spec((D, 3 * D)),
    ]
    out_specs = [
        pl.BlockSpec((1, TK, D), row),
        pl.BlockSpec((1, TK, D), row),
        pl.BlockSpec((1, 1, D, TK), lambda b, i: (b, i, 0, 0)),
    ]
    out_shape = [
        jax.ShapeDtypeStruct((B, S, D), bf16),
        jax.ShapeDtypeStruct((B, S, D), bf16),
        jax.ShapeDtypeStruct((B, S // TK, D, TK), bf16),
    ]
    args = [x, mod_l.reshape(B, 1, 6, D), w_qkv]
    if w_f is None:
        body, name = _qkv_kernel, "attn_qkv"
    else:
        body, name = _qkvf_kernel, "attn_qkv_forget"
        in_specs += [_const_spec((D, LANES)), _const_spec((N_HEADS, 1))]
        out_specs += [pl.BlockSpec((1, N_HEADS, TK), lambda b, i: (b, 0, i))]
        out_shape += [jax.ShapeDtypeStruct((B, N_HEADS, S), f32)]
        args += [w_f, b_f]
    return pl.pallas_call(
        body,
        grid=(B, S // TK),
        in_specs=in_specs,
        out_specs=out_specs,
        out_shape=out_shape,
        compiler_params=_params("arbitrary", "arbitrary"),
        name=name,
    )(*args)


def _cumsum_kernel(lf_ref, f_ref):
    S = lf_ref.shape[2]
    r = lax.broadcasted_iota(jnp.int32, (LANES, LANES), 0)
    c = lax.broadcasted_iota(jnp.int32, (LANES, LANES), 1)
    upper = (r <= c).astype(f32)
    carry = jnp.zeros((N_HEADS, 1), f32)
    for j in range(S // LANES):
        cols = slice(j * LANES, (j + 1) * LANES)
        cs = jnp.dot(lf_ref[0, :, cols], upper, preferred_element_type=f32,
                     precision=lax.Precision.HIGHEST) + carry
        f_ref[0, :, cols] = cs
        carry = cs[:, LANES - 1:LANES]


def _cumsum(lf):
    B, H, S = lf.shape
    return pl.pallas_call(
        _cumsum_kernel,
        grid=(B,),
        in_specs=[pl.BlockSpec((1, H, S), lambda b: (b, 0, 0))],
        out_specs=pl.BlockSpec((1, H, S), lambda b: (b, 0, 0)),
        out_shape=jax.ShapeDtypeStruct((B, H, S), f32),
        compiler_params=_params("arbitrary"),
        name="forget_cumsum",
    )(lf)


def _head_mask(q2, hh):
    lane = lax.broadcasted_iota(jnp.int32, q2.shape, 1)
    return jnp.where((lane // HEAD_DIM) == hh, q2, jnp.zeros_like(q2))


def _key_block(k_ref, j):
    return k_ref[0, pl.ds(pl.multiple_of(j * TK, TK), TK), :]


def _fox_attn_kernel(q_ref, k_ref, vt_ref, fq_ref, fk_ref, o_ref, fcol_ref, s_ref, p_ref, acc_ref):
    i = pl.program_id(2)
    n_kb = fk_ref.shape[3] // TK

    @pl.when(i == 0)
    def _():
        for hh in range(HEADS_PER_STEP):
            for j in range(n_kb):
                rowv = fk_ref[0, 0, hh:hh + 1, j * TK:(j + 1) * TK]
                fcol_ref[hh, j * TK:(j + 1) * TK, :] = jnp.broadcast_to(rowv, (LANES, TK)).T

    q2 = q_ref[0]
    qhs = [_head_mask(q2, hh) for hh in range(HEADS_PER_STEP)]
    frows = [fq_ref[0, 0, hh:hh + 1, :] for hh in range(HEADS_PER_STEP)]
    rel = (lax.broadcasted_iota(jnp.int32, (TK, TQ), 0) - lax.broadcasted_iota(jnp.int32, (TK, TQ), 1))
    n_full = i * (TQ // TK)
    for hh in range(HEADS_PER_STEP):
        acc_ref[hh] = jnp.zeros((HEAD_DIM + ONES_ROWS, TQ), f32)
    ones = jnp.ones((ONES_ROWS, TK), bf16)

    def scores(j, slot, q0=0):
        kb = _key_block(k_ref, j)
        for hh in range(HEADS_PER_STEP):
            s_ref[slot, hh, :, q0:] = lax.dot_general(kb, qhs[hh][q0:], _NT, preferred_element_type=f32)

    def softmax(j, slot, stats, masked, q0=0):
        out = []
        for hh in range(HEADS_PER_STEP):
            m, _ = stats[hh]
            fk = fcol_ref[hh, pl.ds(pl.multiple_of(j * TK, TK), TK), :]
            s = (s_ref[slot, hh, :, q0:] + frows[hh][:, q0:]) - jnp.concatenate([fk] * ((TQ - q0) // LANES), axis=1)
            if masked:
                s = jnp.where(rel[:, q0:] <= i * TQ - j * TK, s, NEG_INF)
            m_new = jnp.maximum(m[:, q0:], jnp.max(s, axis=0, keepdims=True))
            p_ref[slot, hh, :, q0:] = jnp.exp2((s - m_new).astype(bf16))
            alpha = jnp.exp2(m[:, q0:] - m_new)
            if q0:
                m_new = jnp.concatenate([m[:, :q0], m_new], axis=1)
                alpha = jnp.concatenate([jnp.ones((1, q0), f32), alpha], axis=1)
            out.append((m_new, alpha))
        return tuple(out)

    def pv(j, slot, stats, q0=0):
        for hh in range(HEADS_PER_STEP):
            vt = vt_ref[0, j, hh * HEAD_DIM:(hh + 1) * HEAD_DIM, :]
            upd = jnp.dot(jnp.concatenate([vt, ones], axis=0), p_ref[slot, hh, :, q0:], preferred_element_type=f32)
            acc_ref[hh, :, q0:] = stats[hh][1][:, q0:] * acc_ref[hh, :, q0:] + upd

    def finish():
        outs = []
        for hh in range(HEADS_PER_STEP):
            acc = acc_ref[hh]
            outs.append((acc[:HEAD_DIM] * (1.0 / acc[HEAD_DIM:HEAD_DIM + 1])).T)
        o_ref[0] = jnp.concatenate(outs, axis=1).astype(bf16)

    row = lambda v: jnp.full((1, TQ), v, f32)
    stats0 = tuple((row(NEG_INF), row(0.0)) for _ in range(HEADS_PER_STEP))
    d0, d1 = n_full, n_full + 1

    @pl.when(i == 0)
    def _():
        scores(d0, 0)
        scores(d1, 1, TK)
        st = softmax(d0, 0, stats0, True)
        pv(d0, 0, st)
        st = softmax(d1, 1, st, True, TK)
        pv(d1, 1, st, TK)
        finish()

    @pl.when(i > 0)
    def _():
        scores(d0, 0)
        scores(d1, 1, TK)
        st = softmax(d0, 0, stats0, True)
        pv(d0, 0, st)
        scores(0, 0)
        st = softmax(d1, 1, st, True, TK)

        pv(d1, 1, st, TK)
        scores(1, 1)
        st = softmax(0, 0, st, False)

        def two_ticks(u, st):
            j = 2 * u
            pv(j, 0, st)
            scores(j + 2, 0)
            st = softmax(j + 1, 1, st, False)
            pv(j + 1, 1, st)
            scores(j + 3, 1)
            return softmax(j + 2, 0, st, False)

        st = lax.fori_loop(0, i - 1, two_ticks, st)
        pv(n_full - 2, 0, st)
        st = softmax(n_full - 1, 1, st, False)
        pv(n_full - 1, 1, st)
        finish()


def _fox_attn(q, k, vt, F):
    B, S, D = q.shape
    n_hp = N_HEADS // HEADS_PER_STEP
    F4 = F.reshape(B, n_hp, HEADS_PER_STEP, S)
    return pl.pallas_call(
        _fox_attn_kernel,
        grid=(B, n_hp, S // TQ),
        in_specs=[
            pl.BlockSpec((1, TQ, LANES), lambda b, h, i: (b, i, h)),
            pl.BlockSpec((1, S, LANES), lambda b, h, i: (b, 0, h)),
            pl.BlockSpec((1, S // TK, LANES, TK), lambda b, h, i: (b, 0, h, 0)),
            pl.BlockSpec((1, 1, HEADS_PER_STEP, TQ), lambda b, h, i: (b, h, 0, i)),
            pl.BlockSpec((1, 1, HEADS_PER_STEP, S), lambda b, h, i: (b, h, 0, 0)),
        ],
        out_specs=pl.BlockSpec((1, TQ, LANES), lambda b, h, i: (b, i, h)),
        out_shape=jax.ShapeDtypeStruct((B, S, D), bf16),
        scratch_shapes=[
            pltpu.VMEM((HEADS_PER_STEP, S, LANES), f32),
            pltpu.VMEM((2, HEADS_PER_STEP, TK, TQ), f32),
            pltpu.VMEM((2, HEADS_PER_STEP, TK, TQ), bf16),
            pltpu.VMEM((HEADS_PER_STEP, HEAD_DIM + ONES_ROWS, TQ), f32),
        ],
        compiler_params=_params("arbitrary", "arbitrary", "arbitrary"),
        name="fox_attention",
    )(q, k, vt, F4, F4)


def _sb_attn_kernel(q_ref, k_ref, vt_ref, o_ref):
    i = pl.program_id(2)
    q2 = q_ref[0]
    qhs = [_head_mask(q2, hh) for hh in range(SB_HEADS)]
    rel = (lax.broadcasted_iota(jnp.int32, (TK, TQ), 0) - lax.broadcasted_iota(jnp.int32, (TK, TQ), 1))
    r = lax.broadcasted_iota(jnp.int32, (TK, TK), 0)
    c = lax.broadcasted_iota(jnp.int32, (TK, TK), 1)
    suffix = (r <= c).astype(bf16)

    def step(j, carry, diag, q0=0):
        kb = _key_block(k_ref, j)
        out = []
        for hh in range(SB_HEADS):
            tail, acc = carry[hh]
            z = lax.dot_general(kb, qhs[hh][q0:], _NT, preferred_element_type=f32)
            log_1m = -(jnp.maximum(z, 0.0) + jnp.log2(1.0 + jnp.exp2(-jnp.abs(z))))
            if diag:
                valid = rel[:, q0:] < i * TQ - j * TK
                log_1m = jnp.where(valid, log_1m, 0.0)
            incl = jnp.dot(suffix, log_1m.astype(bf16), preferred_element_type=f32)
            a = jnp.exp2((z + incl) + tail[:, q0:])
            if diag:
                a = jnp.where(valid, a, 0.0)
            vt = vt_ref[0, j, hh * HEAD_DIM:(hh + 1) * HEAD_DIM, :]
            upd = jnp.dot(vt, a.astype(bf16), preferred_element_type=f32)
            new_tail, new_acc = tail[:, q0:] + incl[0:1, :], acc[:, q0:] + upd
            if q0:
                new_tail = jnp.concatenate([tail[:, :q0], new_tail], axis=1)
                new_acc = jnp.concatenate([acc[:, :q0], new_acc], axis=1)
            out.append((new_tail, new_acc))
        return tuple(out)

    def max_tail(carry):
        return functools.reduce(jnp.maximum, [jnp.max(tail) for tail, _ in carry])

    def finish(carry):
        o_ref[0] = jnp.concatenate([acc.T for _, acc in carry], axis=1).astype(bf16)

    init = tuple((jnp.zeros((1, TQ), f32), jnp.zeros((HEAD_DIM, TQ), f32)) for _ in range(SB_HEADS))
    n_full = i * (TQ // TK)

    def diagonal(carry):
        for d in reversed(range(TQ // TK)):
            carry = step(n_full + d, carry, diag=True, q0=d * TK)
        return carry

    @pl.when(i == 0)
    def _():
        finish(diagonal(init))

    @pl.when(i > 0)
    def _():
        carry = step(n_full - 1, diagonal(init), diag=False)

        def more(state):
            t, worst, _ = state
            return jnp.logical_and(t < n_full, worst >= SB_UNDERFLOW)

        def body(state):
            t, _, cr = state
            cr = step(n_full - 1 - t, cr, diag=False)
            return t + 1, max_tail(cr), cr

        _, _, carry = lax.while_loop(more, body, (jnp.int32(1), max_tail(carry), carry))
        finish(carry)


def _sb_attn(q, k, vt):
    B, S, D = q.shape
    width = SB_HEADS * HEAD_DIM
    return pl.pallas_call(
        _sb_attn_kernel,
        grid=(B, N_HEADS // SB_HEADS, S // TQ),
        in_specs=[
            pl.BlockSpec((1, TQ, width), lambda b, h, i: (b, i, h)),
            pl.BlockSpec((1, S, width), lambda b, h, i: (b, 0, h)),
            pl.BlockSpec((1, S // TK, width, TK), lambda b, h, i: (b, 0, h, 0)),
        ],
        out_specs=pl.BlockSpec((1, TQ, width), lambda b, h, i: (b, i, h)),
        out_shape=jax.ShapeDtypeStruct((B, S, D), bf16),
        compiler_params=_params("arbitrary", "arbitrary", "arbitrary"),
        name="stickbreak_attention",
    )(q, k, vt)


def _glu_kernel(x_ref, mod_ref, w_ref, b_ref, o_ref):
    h = _modulated(x_ref, mod_ref)
    ag = jnp.dot(h, w_ref[...], preferred_element_type=f32) + b_ref[...]
    o_ref[0] = ag[:, :D_MODEL] * _sigmoid(ag[:, D_MODEL:])


def _glu(x, mod_l, w_in, b_in):
    B, S, D = x.shape
    row = lambda b, i: (b, i, 0)
    return pl.pallas_call(
        _glu_kernel,
        grid=(B, S // TM),
        in_specs=[
            pl.BlockSpec((1, TM, D), row),
            pl.BlockSpec((1, 1, 6, D), lambda b, i: (b, 0, 0, 0)),
            _const_spec((D, 2 * D)),
            _const_spec((1, 2 * D)),
        ],
        out_specs=pl.BlockSpec((1, TM, D), row),
        out_shape=jax.ShapeDtypeStruct((B, S, D), f32),
        compiler_params=_params("arbitrary", "arbitrary"),
        name="conv_glu",
    )(x, mod_l.reshape(B, 1, 6, D), w_in, b_in)


def _dwconv_kernel(halo_ref, y_ref, dw_ref, dwb_ref, lng_ref, lnb_ref, o_ref, ybuf_ref, z_ref):
    i = pl.program_id(1)
    halo = halo_ref[0]
    ybuf_ref[0:CONV_HALO, :] = jnp.where(i > 0, halo, jnp.zeros_like(halo))
    ybuf_ref[CONV_HALO:CONV_HALO + TM, :] = y_ref[0]
    ybuf_ref[CONV_HALO + TM:, :] = jnp.zeros((SUBLANES, D_MODEL), f32)
    first = CONV_HALO - (CONV_WIDTH - 1)
    for cb in range(D_MODEL // LANES):
        cols = slice(cb * LANES, (cb + 1) * LANES)
        acc = jnp.broadcast_to(dwb_ref[:, cols], (TM, LANES))
        for r in range(SUBLANES):
            z = None
            for a in range((CONV_HALO + SUBLANES) // SUBLANES):
                k = SUBLANES * a + r - first
                if 0 <= k < CONV_WIDTH:
                    term = ybuf_ref[SUBLANES * a:SUBLANES * a + TM + SUBLANES, cols] * dw_ref[k:k + 1, cols]
                    z = term if z is None else z + term
            if r == 0:
                acc = acc + z[:TM]
            else:
                z_ref[r] = z
                acc = acc + z_ref[r, r:r + TM, :]
        ybuf_ref[CONV_HALO:CONV_HALO + TM, cols] = acc
    yn = _ln(ybuf_ref[CONV_HALO:CONV_HALO + TM, :], lng_ref[...], lnb_ref[...])
    o_ref[0] = (yn * _sigmoid(yn)).astype(bf16)


def _dwconv(y, dw, dw_b, ln_g, ln_b):
    B, S, D = y.shape
    per = TM // CONV_HALO
    row = lambda b, i: (b, i, 0)
    return pl.pallas_call(
        _dwconv_kernel,
        grid=(B, S // TM),
        in_specs=[
            pl.BlockSpec((1, CONV_HALO, D), lambda b, i: (b, jnp.maximum(i * per - 1, 0), 0)),
            pl.BlockSpec((1, TM, D), row),
            _const_spec((CONV_WIDTH, D)),
            _const_spec((1, D)),
            _const_spec((1, D)),
            _const_spec((1, D)),
        ],
        out_specs=pl.BlockSpec((1, TM, D), row),
        out_shape=jax.ShapeDtypeStruct((B, S, D), bf16),
        scratch_shapes=[
            pltpu.VMEM((CONV_HALO + TM + SUBLANES, D), f32),
            pltpu.VMEM((SUBLANES, TM + SUBLANES, LANES), f32),
        ],
        compiler_params=_params("arbitrary", "arbitrary"),
        name="conv_dw_ln_swish",
    )(y, y, dw, dw_b, ln_g, ln_b)


def kernel(x, c, mod_w, mod_b, ln1_g, ln1_b, ln2_g, ln2_b, ffn_w_in, ffn_w_out, gm_w_in, gm_b_in, gm_ln_g, gm_ln_b, gm_w_s, gm_b_s, gm_w_out, fox_w_in, fox_b_f, fox_w_out, sb_w_in, sb_w_out, cv_w_in, cv_b_in, cv_dw, cv_dw_b, cv_ln_g, cv_ln_b, cv_w_out, cv_b_out):
    B, S, D = x.shape
    assert (B, S, D) == (16, 4096, D_MODEL) and S % TM_POST == 0 and TM_POST % TM == 0
    assert TQ == 2 * TK and TM == TK
    row = lambda v: v.reshape(1, -1)
    mod = _modulation(c, mod_w, mod_b)
    zero_bias = jnp.zeros((1, D), f32)
    for l in range(DEPTH):
        m, j = l % 4, l // 4
        if m == 0:
            bs_full = jnp.repeat(gm_b_s[j].T, GM_GROUP_DIM, axis=1)
            ypre = _gmlp_pre(x, mod[l], gm_w_in[j].astype(bf16), row(gm_b_in[j]), row(gm_ln_g[j]),
                             row(gm_ln_b[j]), gm_w_s[j], bs_full)
            w_mo, b_mo = gm_w_out[j], zero_bias
        elif m == 1:
            w = fox_w_in[j]
            w_f = jnp.pad(w[:, 3 * D:], ((0, 0), (0, LANES - N_HEADS))).astype(bf16)
            q, k, vt, lf = _attn_in(x, mod[l], w[:, :3 * D].astype(bf16), w_f, fox_b_f[j].reshape(N_HEADS, 1))
            ypre = _fox_attn(q, k, vt, _cumsum(lf))
            w_mo, b_mo = fox_w_out[j], zero_bias
        elif m == 2:
            q, k, vt = _attn_in(x, mod[l], sb_w_in[j].astype(bf16))
            ypre = _sb_attn(q, k, vt)
            w_mo, b_mo = sb_w_out[j], zero_bias
        else:
            y = _glu(x, mod[l], cv_w_in[j].astype(bf16), row(cv_b_in[j]))
            ypre = _dwconv(y, cv_dw[j], row(cv_dw_b[j]), row(cv_ln_g[j]), row(cv_ln_b[j]))
            w_mo, b_mo = cv_w_out[j], row(cv_b_out[j])
        x = _post(x, ypre, mod[l], w_mo.astype(bf16), b_mo, row(ln1_g[l]), row(ln1_b[l]),
                  ffn_w_in[l].astype(bf16), ffn_w_out[l].astype(bf16), row(ln2_g[l]), row(ln2_b[l]))
    return x
```

```python
import functools

import jax
import jax.numpy as jnp
from jax import lax
from jax.experimental import pallas as pl
from jax.experimental.pallas import tpu as pltpu

f32 = jnp.float32
bf16 = jnp.bfloat16

D_MODEL = 1024
DEPTH = 4
HEAD_DIM = 64
N_HEADS = D_MODEL // HEAD_DIM
HEADS_PER_STEP = 2
SB_HEADS = 4
LANES = 128
SUBLANES = 8
ONES_ROWS = 16
GM_CHUNK = 128
GM_GROUPS = 8
GM_GROUP_DIM = D_MODEL // GM_GROUPS
CONV_WIDTH = 31
CONV_HALO = 32
FFN_HIDDEN = 2816
DN_ALPHA = (2.0 * DEPTH) ** 0.25
LN_EPS = 1e-5
NEG_INF = -1e30
LOG2E = 1.4426950408889634
Q_SCALE = HEAD_DIM ** -0.5 * LOG2E
SB_UNDERFLOW = -151.0
TM = 256
TM_POST = 512
TQ = 512
TK = 256
VMEM_LIMIT = 56 * 1024 * 1024

_NT = (((1,), (1,)), ((), ()))


def _ln(x, g, b):
    mu = jnp.mean(x, axis=-1, keepdims=True)
    xc = x - mu
    var = jnp.mean(xc * xc, axis=-1, keepdims=True)
    return xc * lax.rsqrt(var + LN_EPS) * g + b


def _sigmoid(x):
    return 1.0 / (1.0 + jnp.exp(-x))


def _log_sigmoid(x):
    return jnp.minimum(x, 0.0) - jnp.log(1.0 + jnp.exp(-jnp.abs(x)))


def _gelu(x):
    return 0.5 * x * (1.0 + lax.erf(x * (2.0 ** -0.5)))


def _params(*sem):
    return pltpu.CompilerParams(dimension_semantics=sem, vmem_limit_bytes=VMEM_LIMIT)


def _const_spec(shape):
    nd = len(shape)
    return pl.BlockSpec(shape, lambda *_: (0,) * nd, pipeline_mode=pl.Buffered(1))


def _mod_kernel(c_ref, w_ref, b_ref, o_ref):
    c = c_ref[...]
    ca = (c * _sigmoid(c)).astype(bf16)
    o_ref[0, 0] = jnp.dot(ca, w_ref[0].astype(bf16), preferred_element_type=f32) + b_ref[0, 0]


def _modulation(c, mod_w, mod_b):
    B, D = c.shape
    L = mod_w.shape[0]
    out = pl.pallas_call(
        _mod_kernel,
        grid=(L, 6),
        in_specs=[
            pl.BlockSpec((B, D), lambda l, j: (0, 0)),
            pl.BlockSpec((1, D, D), lambda l, j: (l, 0, j)),
            pl.BlockSpec((1, 1, 1, D), lambda l, j: (l, j, 0, 0)),
        ],
        out_specs=pl.BlockSpec((1, 1, B, D), lambda l, j: (l, j, 0, 0)),
        out_shape=jax.ShapeDtypeStruct((L, 6, B, D), f32),
        compiler_params=_params("arbitrary", "arbitrary"),
        name="adaln_mod",
    )(c, mod_w, mod_b.reshape(L, 6, 1, D))
    return out.transpose(0, 2, 1, 3)


def _post_kernel(x_ref, y_ref, mod_ref, wmo_ref, bmo_ref, ln1g_ref, ln1b_ref, win_ref, wout_ref,
                 ln2g_ref, ln2b_ref, o_ref):
    mod = mod_ref[0, 0]
    g1, sh2, sc2, g2 = mod[2:3], mod[3:4], mod[4:5], mod[5:6]
    for r in range(TM_POST // TM):
        rows = slice(r * TM, (r + 1) * TM)
        x = x_ref[0, rows, :]
        y = jnp.dot(y_ref[0, rows, :], wmo_ref[...], preferred_element_type=f32) + bmo_ref[...]
        x1 = _ln(DN_ALPHA * x + (1.0 + g1) * y, ln1g_ref[...], ln1b_ref[...])
        h = (x1 * (1.0 + sc2) + sh2).astype(bf16)
        gu = jnp.dot(h, win_ref[...], preferred_element_type=f32)
        g = gu[:, :FFN_HIDDEN]
        u = gu[:, FFN_HIDDEN:]
        a = (g * _sigmoid(g) * u).astype(bf16)
        y2 = jnp.dot(a, wout_ref[...], preferred_element_type=f32)
        o_ref[0, rows, :] = _ln(DN_ALPHA * x1 + (1.0 + g2) * y2, ln2g_ref[...], ln2b_ref[...])


def _post(x, ypre, mod_l, w_mix_out, b_mix_out, ln1g, ln1b, w_in, w_out, ln2g, ln2b):
    B, S, D = x.shape
    row = lambda b, i: (b, i, 0)
    return pl.pallas_call(
        _post_kernel,
        grid=(B, S // TM_POST),
        in_specs=[
            pl.BlockSpec((1, TM_POST, D), row),
            pl.BlockSpec((1, TM_POST, D), row),
            pl.BlockSpec((1, 1, 6, D), lambda b, i: (b, 0, 0, 0)),
            _const_spec((D, D)),
            _const_spec((1, D)),
            _const_spec((1, D)),
            _const_spec((1, D)),
            _const_spec((D, 2 * FFN_HIDDEN)),
            _const_spec((FFN_HIDDEN, D)),
            _const_spec((1, D)),
            _const_spec((1, D)),
        ],
        out_specs=pl.BlockSpec((1, TM_POST, D), row),
        out_shape=jax.ShapeDtypeStruct((B, S, D), f32),
        compiler_params=_params("arbitrary", "arbitrary"),
        name="mixout_ln_ffn_ln",
    )(x, ypre, mod_l.reshape(B, 1, 6, D), w_mix_out, b_mix_out, ln1g, ln1b, w_in, w_out, ln2g, ln2b)


def _gmlp_kernel(x_ref, mod_ref, win_ref, bin_ref, lng_ref, lnb_ref, ws_ref, bs_ref, o_ref):
    h = _modulated(x_ref, mod_ref)
    z = _gelu(jnp.dot(h, win_ref[...], preferred_element_type=f32) + bin_ref[...])
    u = z[:, :D_MODEL]
    v = _ln(z[:, D_MODEL:], lng_ref[...], lnb_ref[...]).astype(bf16)
    r = lax.broadcasted_iota(jnp.int32, (GM_CHUNK, GM_CHUNK), 0)
    c = lax.broadcasted_iota(jnp.int32, (GM_CHUNK, GM_CHUNK), 1)
    for g in range(GM_GROUPS):
        cols = slice(g * GM_GROUP_DIM, (g + 1) * GM_GROUP_DIM)
        wm = jnp.where(r >= c, ws_ref[g], 0.0).astype(bf16)
        for n in range(TM // GM_CHUNK):
            rows = slice(n * GM_CHUNK, (n + 1) * GM_CHUNK)
            sv = jnp.dot(wm, v[rows, cols], preferred_element_type=f32) + bs_ref[:, cols]
            o_ref[0, rows, cols] = (u[rows, cols] * sv).astype(bf16)


def _gmlp_pre(x, mod_l, w_in, b_in, ln_g, ln_b, w_s, bs_full):
    B, S, D = x.shape
    row = lambda b, i: (b, i, 0)
    return pl.pallas_call(
        _gmlp_kernel,
        grid=(B, S // TM),
        in_specs=[
            pl.BlockSpec((1, TM, D), row),
            pl.BlockSpec((1, 1, 6, D), lambda b, i: (b, 0, 0, 0)),
            _const_spec((D, 2 * D)),
            _const_spec((1, 2 * D)),
            _const_spec((1, D)),
            _const_spec((1, D)),
            _const_spec((GM_GROUPS, GM_CHUNK, GM_CHUNK)),
            _const_spec((GM_CHUNK, D)),
        ],
        out_specs=pl.BlockSpec((1, TM, D), row),
        out_shape=jax.ShapeDtypeStruct((B, S, D), bf16),
        compiler_params=_params("arbitrary", "arbitrary"),
        name="gmlp_pre",
    )(x, mod_l.reshape(B, 1, 6, D), w_in, b_in, ln_g, ln_b, w_s, bs_full)


def _modulated(x_ref, mod_ref):
    mod = mod_ref[0, 0]
    return (x_ref[0] * (1.0 + mod[1:2]) + mod[0:1]).astype(bf16)


def _qkv_store(h, w_ref, q_ref, k_ref, vt_ref):
    qkv = jnp.dot(h, w_ref[...], preferred_element_type=f32)
    q_ref[0] = (qkv[:, :D_MODEL] * Q_SCALE).T.astype(bf16)
    k_ref[0] = qkv[:, D_MODEL:2 * D_MODEL].astype(bf16)
    vt_ref[0, 0] = qkv[:, 2 * D_MODEL:].T.astype(bf16)


def _qkv_kernel(x_ref, mod_ref, w_ref, q_ref, k_ref, vt_ref):
    _qkv_store(_modulated(x_ref, mod_ref), w_ref, q_ref, k_ref, vt_ref)


def _qkvf_kernel(x_ref, mod_ref, w_ref, wf_ref, bf_ref, q_ref, k_ref, vt_ref, lf_ref):
    h = _modulated(x_ref, mod_ref)
    _qkv_store(h, w_ref, q_ref, k_ref, vt_ref)
    f = jnp.dot(h, wf_ref[...], preferred_element_type=f32)
    ft = f.T[:N_HEADS] + bf_ref[...]
    lf_ref[0] = _log_sigmoid(ft) * LOG2E


def _attn_in(x, mod_l, w_qkv, w_f=None, b_f=None):
    B, S, D = x.shape
    row = lambda b, i: (b, i, 0)
    in_specs = [
        pl.BlockSpec((1, TK, D), row),
        pl.BlockSpec((1, 1, 6, D), lambda b, i: (b, 0, 0, 0)),
        _const_spec((D, 3 * D)),
    ]
    out_specs = [
        pl.BlockSpec((1, D, TK), lambda b, i: (b, 0, i)),
        pl.BlockSpec((1, TK, D), row),
        pl.BlockSpec((1, 1, D, TK), lambda b, i: (b, i, 0, 0)),
    ]
    out_shape = [
        jax.ShapeDtypeStruct((B, D, S), bf16),
        jax.ShapeDtypeStruct((B, S, D), bf16),
        jax.ShapeDtypeStruct((B, S // TK, D, TK), bf16),
    ]
    args = [x, mod_l.reshape(B, 1, 6, D), w_qkv]
    if w_f is None:
        body, name = _qkv_kernel, "attn_qkv"
    else:
        body, name = _qkvf_kernel, "attn_qkv_forget"
        in_specs += [_const_spec((D, LANES)), _const_spec((N_HEADS, 1))]
        out_specs += [pl.BlockSpec((1, N_HEADS, TK), lambda b, i: (b, 0, i))]
        out_shape += [jax.ShapeDtypeStruct((B, N_HEADS, S), f32)]
        args += [w_f, b_f]
    return pl.pallas_call(
        body,
        grid=(B, S // TK),
        in_specs=in_specs,
        out_specs=out_specs,
        out_shape=out_shape,
        compiler_params=_params("arbitrary", "arbitrary"),
        name=name,
    )(*args)


def _cumsum_kernel(lf_ref, f_ref):
    S = lf_ref.shape[2]
    r = lax.broadcasted_iota(jnp.int32, (LANES, LANES), 0)
    c = lax.broadcasted_iota(jnp.int32, (LANES, LANES), 1)
    upper = (r <= c).astype(f32)
    carry = jnp.zeros((N_HEADS, 1), f32)
    for j in range(S // LANES):
        cols = slice(j * LANES, (j + 1) * LANES)
        cs = jnp.dot(lf_ref[0, :, cols], upper, preferred_element_type=f32,
                     precision=lax.Precision.HIGHEST) + carry
        f_ref[0, :, cols] = cs
        carry = cs[:, LANES - 1:LANES]


def _cumsum(lf):
    B, H, S = lf.shape
    return pl.pallas_call(
        _cumsum_kernel,
        grid=(B,),
        in_specs=[pl.BlockSpec((1, H, S), lambda b: (b, 0, 0))],
        out_specs=pl.BlockSpec((1, H, S), lambda b: (b, 0, 0)),
        out_shape=jax.ShapeDtypeStruct((B, H, S), f32),
        compiler_params=_params("arbitrary"),
        name="forget_cumsum",
    )(lf)


def _head_mask(qt, hh):
    feat = lax.broadcasted_iota(jnp.int32, qt.shape, 0)
    return jnp.where((feat // HEAD_DIM) == hh, qt, jnp.zeros_like(qt))


def _key_block(k_ref, j):
    return k_ref[0, pl.ds(pl.multiple_of(j * TK, TK), TK), :]


def _fox_attn_kernel(q_ref, k_ref, vt_ref, fq_ref, fk_ref, o_ref, fcol_ref, s_ref, p_ref, acc_ref):
    i = pl.program_id(2)
    n_kb = fk_ref.shape[3] // TK

    @pl.when(i == 0)
    def _():
        for hh in range(HEADS_PER_STEP):
            for j in range(n_kb):
                rowv = fk_ref[0, 0, hh:hh + 1, j * TK:(j + 1) * TK]
                fcol_ref[hh, j * TK:(j + 1) * TK, :] = jnp.broadcast_to(rowv, (LANES, TK)).T

    q2 = q_ref[0]
    qhs = [_head_mask(q2, hh) for hh in range(HEADS_PER_STEP)]
    frows = [fq_ref[0, 0, hh:hh + 1, :] for hh in range(HEADS_PER_STEP)]
    rel = (lax.broadcasted_iota(jnp.int32, (TK, TQ), 0) - lax.broadcasted_iota(jnp.int32, (TK, TQ), 1))
    n_full = i * (TQ // TK)
    for hh in range(HEADS_PER_STEP):
        acc_ref[hh] = jnp.zeros((HEAD_DIM + ONES_ROWS, TQ), f32)
    ones = jnp.ones((ONES_ROWS, TK), bf16)

    def scores(j, slot, q0=0):
        kb = _key_block(k_ref, j)
        for hh in range(HEADS_PER_STEP):
            s_ref[slot, hh, :, q0:] = jnp.dot(kb, qhs[hh][:, q0:], preferred_element_type=f32)

    def softmax(j, slot, stats, masked, q0=0):
        out = []
        for hh in range(HEADS_PER_STEP):
            m, _ = stats[hh]
            fk = fcol_ref[hh, pl.ds(pl.multiple_of(j * TK, TK), TK), :]
            s = (s_ref[slot, hh, :, q0:] + frows[hh][:, q0:]) - jnp.concatenate([fk] * ((TQ - q0) // LANES), axis=1)
            if masked:
                s = jnp.where(rel[:, q0:] <= i * TQ - j * TK, s, NEG_INF)
            m_new = jnp.maximum(m[:, q0:], jnp.max(s, axis=0, keepdims=True))
            p_ref[slot, hh, :, q0:] = jnp.exp2((s - m_new).astype(bf16))
            alpha = jnp.exp2(m[:, q0:] - m_new)
            if q0:
                m_new = jnp.concatenate([m[:, :q0], m_new], axis=1)
                alpha = jnp.concatenate([jnp.ones((1, q0), f32), alpha], axis=1)
            out.append((m_new, alpha))
        return tuple(out)

    def pv(j, slot, stats, q0=0):
        for hh in range(HEADS_PER_STEP):
            vt = vt_ref[0, j, hh * HEAD_DIM:(hh + 1) * HEAD_DIM, :]
            upd = jnp.dot(jnp.concatenate([vt, ones], axis=0), p_ref[slot, hh, :, q0:], preferred_element_type=f32)
            acc_ref[hh, :, q0:] = stats[hh][1][:, q0:] * acc_ref[hh, :, q0:] + upd

    def finish():
        outs = []
        for hh in range(HEADS_PER_STEP):
            acc = acc_ref[hh]
            outs.append((acc[:HEAD_DIM] * (1.0 / acc[HEAD_DIM:HEAD_DIM + 1])).T)
        o_ref[0] = jnp.concatenate(outs, axis=1).astype(bf16)

    row = lambda v: jnp.full((1, TQ), v, f32)
    stats0 = tuple((row(NEG_INF), row(0.0)) for _ in range(HEADS_PER_STEP))
    d0, d1 = n_full, n_full + 1

    @pl.when(i == 0)
    def _():
        scores(d0, 0)
        scores(d1, 1, TK)
        st = softmax(d0, 0, stats0, True)
        pv(d0, 0, st)
        st = softmax(d1, 1, st, True, TK)
        pv(d1, 1, st, TK)
        finish()

    @pl.when(i > 0)
    def _():
        scores(d0, 0)
        scores(d1, 1, TK)
        st = softmax(d0, 0, stats0, True)
        pv(d0, 0, st)
        scores(0, 0)
        st = softmax(d1, 1, st, True, TK)

        pv(d1, 1, st, TK)
        scores(1, 1)
        st = softmax(0, 0, st, False)

        def two_ticks(u, st):
            j = 2 * u
            pv(j, 0, st)
            scores(j + 2, 0)
            st = softmax(j + 1, 1, st, False)
            pv(j + 1, 1, st)
            scores(j + 3, 1)
            return softmax(j + 2, 0, st, False)

        st = lax.fori_loop(0, i - 1, two_ticks, st)
        pv(n_full - 2, 0, st)
        st = softmax(n_full - 1, 1, st, False)
        pv(n_full - 1, 1, st)
        finish()


def _fox_attn(qt, k, vt, F):
    B, S, D = k.shape
    n_hp = N_HEADS // HEADS_PER_STEP
    F4 = F.reshape(B, n_hp, HEADS_PER_STEP, S)
    return pl.pallas_call(
        _fox_attn_kernel,
        grid=(B, n_hp, S // TQ),
        in_specs=[
            pl.BlockSpec((1, LANES, TQ), lambda b, h, i: (b, h, i)),
            pl.BlockSpec((1, S, LANES), lambda b, h, i: (b, 0, h)),
            pl.BlockSpec((1, S // TK, LANES, TK), lambda b, h, i: (b, 0, h, 0)),
            pl.BlockSpec((1, 1, HEADS_PER_STEP, TQ), lambda b, h, i: (b, h, 0, i)),
            pl.BlockSpec((1, 1, HEADS_PER_STEP, S), lambda b, h, i: (b, h, 0, 0)),
        ],
        out_specs=pl.BlockSpec((1, TQ, LANES), lambda b, h, i: (b, i, h)),
        out_shape=jax.ShapeDtypeStruct((B, S, D), bf16),
        scratch_shapes=[
            pltpu.VMEM((HEADS_PER_STEP, S, LANES), f32),
            pltpu.VMEM((2, HEADS_PER_STEP, TK, TQ), f32),
            pltpu.VMEM((2, HEADS_PER_STEP, TK, TQ), bf16),
            pltpu.VMEM((HEADS_PER_STEP, HEAD_DIM + ONES_ROWS, TQ), f32),
        ],
        compiler_params=_params("arbitrary", "arbitrary", "arbitrary"),
        name="fox_attention",
    )(qt, k, vt, F4, F4)


def _sb_attn_kernel(q_ref, k_ref, vt_ref, o_ref):
    i = pl.program_id(2)
    q2 = q_ref[0]
    qhs = [_head_mask(q2, hh) for hh in range(SB_HEADS)]
    rel = (lax.broadcasted_iota(jnp.int32, (TK, TQ), 0) - lax.broadcasted_iota(jnp.int32, (TK, TQ), 1))
    r = lax.broadcasted_iota(jnp.int32, (TK, TK), 0)
    c = lax.broadcasted_iota(jnp.int32, (TK, TK), 1)
    suffix = (r <= c).astype(bf16)

    def step(j, carry, diag, q0=0, q1=TQ):
        kb = _key_block(k_ref, j)
        out = []
        for hh in range(SB_HEADS):
            tail, acc = carry[hh]
            z = jnp.dot(kb, qhs[hh][:, q0:q1], preferred_element_type=f32)
            log_1m = -(jnp.maximum(z, 0.0) + jnp.log2(1.0 + jnp.exp2(-jnp.abs(z))))
            if diag:
                valid = rel[:, q0:q1] < i * TQ - j * TK
                log_1m = jnp.where(valid, log_1m, 0.0)
            incl = jnp.dot(suffix, log_1m.astype(bf16), preferred_element_type=f32)
            a = jnp.exp2((z + incl) + tail[:, q0:q1])
            if diag:
                a = jnp.where(valid, a, 0.0)
            vt = vt_ref[0, j, hh * HEAD_DIM:(hh + 1) * HEAD_DIM, :]
            upd = jnp.dot(vt, a.astype(bf16), preferred_element_type=f32)
            new = []
            for old, part in ((tail, tail[:, q0:q1] + incl[0:1, :]), (acc, acc[:, q0:q1] + upd)):
                pieces = ([old[:, :q0]] if q0 else []) + [part] + ([old[:, q1:]] if q1 < TQ else [])
                new.append(jnp.concatenate(pieces, axis=1) if len(pieces) > 1 else part)
            out.append(tuple(new))
        return tuple(out)

    def max_tail(carry):
        return functools.reduce(jnp.maximum, [jnp.max(tail) for tail, _ in carry])

    def finish(carry):
        o_ref[0] = jnp.concatenate([acc.T for _, acc in carry], axis=1).astype(bf16)

    init = tuple((jnp.zeros((1, TQ), f32), jnp.zeros((HEAD_DIM, TQ), f32)) for _ in range(SB_HEADS))
    n_full = i * (TQ // TK)

    def diagonal(carry):
        for d in reversed(range(TQ // TK)):
            carry = step(n_full + d, carry, diag=True, q0=d * TK)
        return carry

    @pl.when(i == 0)
    def _():
        finish(diagonal(init))

    @pl.when(i > 0)
    def _():
        carry = diagonal(init)
        late = functools.reduce(jnp.maximum, [jnp.max(tail[:, TK:]) for tail, _ in carry])
        carry = lax.cond(late < SB_UNDERFLOW,
                         lambda cr: step(n_full - 1, cr, diag=False, q1=TK),
                         lambda cr: step(n_full - 1, cr, diag=False), carry)

        def more(state):
            t, worst, _ = state
            return jnp.logical_and(t < n_full, worst >= SB_UNDERFLOW)

        def body(state):
            t, _, cr = state
            cr = step(n_full - 1 - t, cr, diag=False)
            return t + 1, max_tail(cr), cr

        _, _, carry = lax.while_loop(more, body, (jnp.int32(1), max_tail(carry), carry))
        finish(carry)


def _sb_attn(qt, k, vt):
    B, S, D = k.shape
    width = SB_HEADS * HEAD_DIM
    return pl.pallas_call(
        _sb_attn_kernel,
        grid=(B, N_HEADS // SB_HEADS, S // TQ),
        in_specs=[
            pl.BlockSpec((1, width, TQ), lambda b, h, i: (b, h, i)),
            pl.BlockSpec((1, S, width), lambda b, h, i: (b, 0, h)),
            pl.BlockSpec((1, S // TK, width, TK), lambda b, h, i: (b, 0, h, 0)),
        ],
        out_specs=pl.BlockSpec((1, TQ, width), lambda b, h, i: (b, i, h)),
        out_shape=jax.ShapeDtypeStruct((B, S, D), bf16),
        compiler_params=_params("arbitrary", "arbitrary", "arbitrary"),
        name="stickbreak_attention",
    )(qt, k, vt)


def _glu_kernel(x_ref, mod_ref, w_ref, b_ref, o_ref):
    h = _modulated(x_ref, mod_ref)
    ag = jnp.dot(h, w_ref[...], preferred_element_type=f32) + b_ref[...]
    o_ref[0] = ag[:, :D_MODEL] * _sigmoid(ag[:, D_MODEL:])


def _glu(x, mod_l, w_in, b_in):
    B, S, D = x.shape
    row = lambda b, i: (b, i, 0)
    return pl.pallas_call(
        _glu_kernel,
        grid=(B, S // TM),
        in_specs=[
            pl.BlockSpec((1, TM, D), row),
            pl.BlockSpec((1, 1, 6, D), lambda b, i: (b, 0, 0, 0)),
            _const_spec((D, 2 * D)),
            _const_spec((1, 2 * D)),
        ],
        out_specs=pl.BlockSpec((1, TM, D), row),
        out_shape=jax.ShapeDtypeStruct((B, S, D), f32),
        compiler_params=_params("arbitrary", "arbitrary"),
        name="conv_glu",
    )(x, mod_l.reshape(B, 1, 6, D), w_in, b_in)


def _dwconv_kernel(halo_ref, y_ref, dw_ref, dwb_ref, lng_ref, lnb_ref, o_ref, ybuf_ref, z_ref):
    i = pl.program_id(1)
    halo = halo_ref[0]
    ybuf_ref[0:CONV_HALO, :] = jnp.where(i > 0, halo, jnp.zeros_like(halo))
    ybuf_ref[CONV_HALO:CONV_HALO + TM, :] = y_ref[0]
    ybuf_ref[CONV_HALO + TM:, :] = jnp.zeros((SUBLANES, D_MODEL), f32)
    first = CONV_HALO - (CONV_WIDTH - 1)
    for cb in range(D_MODEL // LANES):
        cols = slice(cb * LANES, (cb + 1) * LANES)
        acc = jnp.broadcast_to(dwb_ref[:, cols], (TM, LANES))
        for r in range(SUBLANES):
            z = None
            for a in range((CONV_HALO + SUBLANES) // SUBLANES):
                k = SUBLANES * a + r - first
                if 0 <= k < CONV_WIDTH:
                    term = ybuf_ref[SUBLANES * a:SUBLANES * a + TM + SUBLANES, cols] * dw_ref[k:k + 1, cols]
                    z = term if z is None else z + term
            if r == 0:
                acc = acc + z[:TM]
            else:
                z_ref[r] = z
                acc = acc + z_ref[r, r:r + TM, :]
        ybuf_ref[CONV_HALO:CONV_HALO + TM, cols] = acc
    yn = _ln(ybuf_ref[CONV_HALO:CONV_HALO + TM, :], lng_ref[...], lnb_ref[...])
    o_ref[0] = (yn * _sigmoid(yn)).astype(bf16)


def _dwconv(y, dw, dw_b, ln_g, ln_b):
    B, S, D = y.shape
    per = TM // CONV_HALO
    row = lambda b, i: (b, i, 0)
    return pl.pallas_call(
        _dwconv_kernel,
        grid=(B, S // TM),
        in_specs=[
            pl.BlockSpec((1, CONV_HALO, D), lambda b, i: (b, jnp.maximum(i * per - 1, 0), 0)),
            pl.BlockSpec((1, TM, D), row),
            _const_spec((CONV_WIDTH, D)),
            _const_spec((1, D)),
            _const_spec((1, D)),
            _const_spec((1, D)),
        ],
        out_specs=pl.BlockSpec((1, TM, D), row),
        out_shape=jax.ShapeDtypeStruct((B, S, D), bf16),
        scratch_shapes=[
            pltpu.VMEM((CONV_HALO + TM + SUBLANES, D), f32),
            pltpu.VMEM((SUBLANES, TM + SUBLANES, LANES), f32),
        ],
        compiler_params=_params("arbitrary", "arbitrary"),
        name="conv_dw_ln_swish",
    )(y, y, dw, dw_b, ln_g, ln_b)


def kernel(x, c, mod_w, mod_b, ln1_g, ln1_b, ln2_g, ln2_b, ffn_w_in, ffn_w_out, gm_w_in, gm_b_in, gm_ln_g, gm_ln_b, gm_w_s, gm_b_s, gm_w_out, fox_w_in, fox_b_f, fox_w_out, sb_w_in, sb_w_out, cv_w_in, cv_b_in, cv_dw, cv_dw_b, cv_ln_g, cv_ln_b, cv_w_out, cv_b_out):
    B, S, D = x.shape
    assert (B, S, D) == (16, 4096, D_MODEL) and S % TM_POST == 0 and TM_POST % TM == 0
    assert TQ == 2 * TK and TM == TK
    row = lambda v: v.reshape(1, -1)
    mod = _modulation(c, mod_w, mod_b)
    zero_bias = jnp.zeros((1, D), f32)
    for l in range(DEPTH):
        m, j = l % 4, l // 4
        if m == 0:
            bs_full = jnp.repeat(gm_b_s[j].T, GM_GROUP_DIM, axis=1)
            ypre = _gmlp_pre(x, mod[l], gm_w_in[j].astype(bf16), row(gm_b_in[j]), row(gm_ln_g[j]),
                             row(gm_ln_b[j]), gm_w_s[j], bs_full)
            w_mo, b_mo = gm_w_out[j], zero_bias
        elif m == 1:
            w = fox_w_in[j]
            w_f = jnp.pad(w[:, 3 * D:], ((0, 0), (0, LANES - N_HEADS))).astype(bf16)
            q, k, vt, lf = _attn_in(x, mod[l], w[:, :3 * D].astype(bf16), w_f, fox_b_f[j].reshape(N_HEADS, 1))
            ypre = _fox_attn(q, k, vt, _cumsum(lf))
            w_mo, b_mo = fox_w_out[j], zero_bias
        elif m == 2:
            q, k, vt = _attn_in(x, mod[l], sb_w_in[j].astype(bf16))
            ypre = _sb_attn(q, k, vt)
            w_mo, b_mo = sb_w_out[j], zero_bias
        else:
            y = _glu(x, mod[l], cv_w_in[j].astype(bf16), row(cv_b_in[j]))
            ypre = _dwconv(y, cv_dw[j], row(cv_dw_b[j]), row(cv_ln_g[j]), row(cv_ln_b[j]))
            w_mo, b_mo = cv_w_out[j], row(cv_b_out[j])
        x = _post(x, ypre, mod[l], w_mo.astype(bf16), b_mo, row(ln1_g[l]), row(ln1_b[l]),
                  ffn_w_in[l].astype(bf16), ffn_w_out[l].astype(bf16), row(ln2_g[l]), row(ln2_b[l]))
    return x
```

```python
import functools

import jax
import jax.numpy as jnp
from jax import lax
from jax.experimental import pallas as pl
from jax.experimental.pallas import tpu as pltpu

f32 = jnp.float32
bf16 = jnp.bfloat16

D_MODEL = 1024
DEPTH = 4
HEAD_DIM = 64
N_HEADS = D_MODEL // HEAD_DIM
HEADS_PER_STEP = 2
SB_HEADS = 4
LANES = 128
SUBLANES = 8
ONES_ROWS = 16
GM_CHUNK = 128
GM_GROUPS = 8
GM_GROUP_DIM = D_MODEL // GM_GROUPS
CONV_WIDTH = 31
CONV_HALO = 32
FFN_HIDDEN = 2816
DN_ALPHA = (2.0 * DEPTH) ** 0.25
LN_EPS = 1e-5
NEG_INF = -1e30
LOG2E = 1.4426950408889634
Q_SCALE = HEAD_DIM ** -0.5 * LOG2E
SB_UNDERFLOW = -151.0
TM = 256
TM_POST = 512
TQ = 512
TK = 256
VMEM_LIMIT = 56 * 1024 * 1024

_NT = (((1,), (1,)), ((), ()))


def _ln(x, g, b):
    mu = jnp.mean(x, axis=-1, keepdims=True)
    xc = x - mu
    var = jnp.mean(xc * xc, axis=-1, keepdims=True)
    return xc * lax.rsqrt(var + LN_EPS) * g + b


def _sigmoid(x):
    return 1.0 / (1.0 + jnp.exp(-x))


def _log_sigmoid(x):
    return jnp.minimum(x, 0.0) - jnp.log(1.0 + jnp.exp(-jnp.abs(x)))


def _gelu(x):
    return 0.5 * x * (1.0 + lax.erf(x * (2.0 ** -0.5)))


def _params(*sem):
    return pltpu.CompilerParams(dimension_semantics=sem, vmem_limit_bytes=VMEM_LIMIT)


def _const_spec(shape):
    nd = len(shape)
    return pl.BlockSpec(shape, lambda *_: (0,) * nd, pipeline_mode=pl.Buffered(1))


def _mod_kernel(c_ref, w_ref, b_ref, o_ref):
    c = c_ref[...]
    ca = (c * _sigmoid(c)).astype(bf16)
    o_ref[0, 0] = jnp.dot(ca, w_ref[0].astype(bf16), preferred_element_type=f32) + b_ref[0, 0]


def _modulation(c, mod_w, mod_b):
    B, D = c.shape
    L = mod_w.shape[0]
    out = pl.pallas_call(
        _mod_kernel,
        grid=(L, 6),
        in_specs=[
            pl.BlockSpec((B, D), lambda l, j: (0, 0)),
            pl.BlockSpec((1, D, D), lambda l, j: (l, 0, j)),
            pl.BlockSpec((1, 1, 1, D), lambda l, j: (l, j, 0, 0)),
        ],
        out_specs=pl.BlockSpec((1, 1, B, D), lambda l, j: (l, j, 0, 0)),
        out_shape=jax.ShapeDtypeStruct((L, 6, B, D), f32),
        compiler_params=_params("arbitrary", "arbitrary"),
        name="adaln_mod",
    )(c, mod_w, mod_b.reshape(L, 6, 1, D))
    return out.transpose(0, 2, 1, 3)


def _post_kernel(x_ref, y_ref, mod_ref, wmo_ref, bmo_ref, ln1g_ref, ln1b_ref, win_ref, wout_ref,
                 ln2g_ref, ln2b_ref, o_ref):
    mod = mod_ref[0, 0]
    g1, sh2, sc2, g2 = mod[2:3], mod[3:4], mod[4:5], mod[5:6]
    for r in range(TM_POST // TM):
        rows = slice(r * TM, (r + 1) * TM)
        x = x_ref[0, rows, :]
        y = jnp.dot(y_ref[0, rows, :], wmo_ref[...], preferred_element_type=f32) + bmo_ref[...]
        x1 = _ln(DN_ALPHA * x + (1.0 + g1) * y, ln1g_ref[...], ln1b_ref[...])
        h = (x1 * (1.0 + sc2) + sh2).astype(bf16)
        gu = jnp.dot(h, win_ref[...], preferred_element_type=f32)
        g = gu[:, :FFN_HIDDEN]
        u = gu[:, FFN_HIDDEN:]
        a = (g * _sigmoid(g) * u).astype(bf16)
        y2 = jnp.dot(a, wout_ref[...], preferred_element_type=f32)
        o_ref[0, rows, :] = _ln(DN_ALPHA * x1 + (1.0 + g2) * y2, ln2g_ref[...], ln2b_ref[...])


def _post(x, ypre, mod_l, w_mix_out, b_mix_out, ln1g, ln1b, w_in, w_out, ln2g, ln2b):
    B, S, D = x.shape
    row = lambda b, i: (b, i, 0)
    return pl.pallas_call(
        _post_kernel,
        grid=(B, S // TM_POST),
        in_specs=[
            pl.BlockSpec((1, TM_POST, D), row),
            pl.BlockSpec((1, TM_POST, D), row),
            pl.BlockSpec((1, 1, 6, D), lambda b, i: (b, 0, 0, 0)),
            _const_spec((D, D)),
            _const_spec((1, D)),
            _const_spec((1, D)),
            _const_spec((1, D)),
            _const_spec((D, 2 * FFN_HIDDEN)),
            _const_spec((FFN_HIDDEN, D)),
            _const_spec((1, D)),
            _const_spec((1, D)),
        ],
        out_specs=pl.BlockSpec((1, TM_POST, D), row),
        out_shape=jax.ShapeDtypeStruct((B, S, D), f32),
        compiler_params=_params("arbitrary", "arbitrary"),
        name="mixout_ln_ffn_ln",
    )(x, ypre, mod_l.reshape(B, 1, 6, D), w_mix_out, b_mix_out, ln1g, ln1b, w_in, w_out, ln2g, ln2b)


def _gmlp_kernel(x_ref, mod_ref, win_ref, bin_ref, lng_ref, lnb_ref, ws_ref, bs_ref, o_ref):
    h = _modulated(x_ref, mod_ref)
    z = _gelu(jnp.dot(h, win_ref[...], preferred_element_type=f32) + bin_ref[...])
    u = z[:, :D_MODEL]
    v = _ln(z[:, D_MODEL:], lng_ref[...], lnb_ref[...]).astype(bf16)
    r = lax.broadcasted_iota(jnp.int32, (GM_CHUNK, GM_CHUNK), 0)
    c = lax.broadcasted_iota(jnp.int32, (GM_CHUNK, GM_CHUNK), 1)
    for g in range(GM_GROUPS):
        cols = slice(g * GM_GROUP_DIM, (g + 1) * GM_GROUP_DIM)
        wm = jnp.where(r >= c, ws_ref[g], 0.0).astype(bf16)
        for n in range(TM // GM_CHUNK):
            rows = slice(n * GM_CHUNK, (n + 1) * GM_CHUNK)
            sv = jnp.dot(wm, v[rows, cols], preferred_element_type=f32) + bs_ref[:, cols]
            o_ref[0, rows, cols] = (u[rows, cols] * sv).astype(bf16)


def _gmlp_pre(x, mod_l, w_in, b_in, ln_g, ln_b, w_s, bs_full):
    B, S, D = x.shape
    row = lambda b, i: (b, i, 0)
    return pl.pallas_call(
        _gmlp_kernel,
        grid=(B, S // TM),
        in_specs=[
            pl.BlockSpec((1, TM, D), row),
            pl.BlockSpec((1, 1, 6, D), lambda b, i: (b, 0, 0, 0)),
            _const_spec((D, 2 * D)),
            _const_spec((1, 2 * D)),
            _const_spec((1, D)),
            _const_spec((1, D)),
            _const_spec((GM_GROUPS, GM_CHUNK, GM_CHUNK)),
            _const_spec((GM_CHUNK, D)),
        ],
        out_specs=pl.BlockSpec((1, TM, D), row),
        out_shape=jax.ShapeDtypeStruct((B, S, D), bf16),
        compiler_params=_params("arbitrary", "arbitrary"),
        name="gmlp_pre",
    )(x, mod_l.reshape(B, 1, 6, D), w_in, b_in, ln_g, ln_b, w_s, bs_full)


def _modulated(x_ref, mod_ref):
    mod = mod_ref[0, 0]
    return (x_ref[0] * (1.0 + mod[1:2]) + mod[0:1]).astype(bf16)


def _qkv_store(h, w_ref, q_ref, k_ref, vt_ref):
    qkv = jnp.dot(h, w_ref[...], preferred_element_type=f32)
    q_ref[0] = (qkv[:, :D_MODEL] * Q_SCALE).T.astype(bf16)
    k_ref[0] = qkv[:, D_MODEL:2 * D_MODEL].astype(bf16)
    vt_ref[0, 0] = qkv[:, 2 * D_MODEL:].T.astype(bf16)


def _qkv_kernel(x_ref, mod_ref, w_ref, q_ref, k_ref, vt_ref):
    _qkv_store(_modulated(x_ref, mod_ref), w_ref, q_ref, k_ref, vt_ref)


def _qkvf_kernel(x_ref, mod_ref, w_ref, wf_ref, bf_ref, q_ref, k_ref, vt_ref, lf_ref):
    h = _modulated(x_ref, mod_ref)
    _qkv_store(h, w_ref, q_ref, k_ref, vt_ref)
    f = jnp.dot(h, wf_ref[...], preferred_element_type=f32)
    ft = f.T[:N_HEADS] + bf_ref[...]
    lf_ref[0] = _log_sigmoid(ft) * LOG2E


def _attn_in(x, mod_l, w_qkv, w_f=None, b_f=None):
    B, S, D = x.shape
    row = lambda b, i: (b, i, 0)
    in_specs = [
        pl.BlockSpec((1, TK, D), row),
        pl.BlockSpec((1, 1, 6, D), lambda b, i: (b, 0, 0, 0)),
        _const_spec((D, 3 * D)),
    ]
    out_specs = [
        pl.BlockSpec((1, D, TK), lambda b, i: (b, 0, i)),
        pl.BlockSpec((1, TK, D), row),
        pl.BlockSpec((1, 1, D, TK), lambda b, i: (b, i, 0, 0)),
    ]
    out_shape = [
        jax.ShapeDtypeStruct((B, D, S), bf16),
        jax.ShapeDtypeStruct((B, S, D), bf16),
        jax.ShapeDtypeStruct((B, S // TK, D, TK), bf16),
    ]
    args = [x, mod_l.reshape(B, 1, 6, D), w_qkv]
    if w_f is None:
        body, name = _qkv_kernel, "attn_qkv"
    else:
        body, name = _qkvf_kernel, "attn_qkv_forget"
        in_specs += [_const_spec((D, LANES)), _const_spec((N_HEADS, 1))]
        out_specs += [pl.BlockSpec((1, N_HEADS, TK), lambda b, i: (b, 0, i))]
        out_shape += [jax.ShapeDtypeStruct((B, N_HEADS, S), f32)]
        args += [w_f, b_f]
    return pl.pallas_call(
        body,
        grid=(B, S // TK),
        in_specs=in_specs,
        out_specs=out_specs,
        out_shape=out_shape,
        compiler_params=_params("arbitrary", "arbitrary"),
        name=name,
    )(*args)


def _cumsum_kernel(lf_ref, f_ref):
    S = lf_ref.shape[2]
    r = lax.broadcasted_iota(jnp.int32, (LANES, LANES), 0)
    c = lax.broadcasted_iota(jnp.int32, (LANES, LANES), 1)
    upper = (r <= c).astype(f32)
    carry = jnp.zeros((N_HEADS, 1), f32)
    for j in range(S // LANES):
        cols = slice(j * LANES, (j + 1) * LANES)
        cs = jnp.dot(lf_ref[0, :, cols], upper, preferred_element_type=f32,
                     precision=lax.Precision.HIGHEST) + carry
        f_ref[0, :, cols] = cs
        carry = cs[:, LANES - 1:LANES]


def _cumsum(lf):
    B, H, S = lf.shape
    return pl.pallas_call(
        _cumsum_kernel,
        grid=(B,),
        in_specs=[pl.BlockSpec((1, H, S), lambda b: (b, 0, 0))],
        out_specs=pl.BlockSpec((1, H, S), lambda b: (b, 0, 0)),
        out_shape=jax.ShapeDtypeStruct((B, H, S), f32),
        compiler_params=_params("arbitrary"),
        name="forget_cumsum",
    )(lf)


def _head_mask(qt, hh):
    feat = lax.broadcasted_iota(jnp.int32, qt.shape, 0)
    return jnp.where((feat // HEAD_DIM) == hh, qt, jnp.zeros_like(qt))


def _key_block(k_ref, j):
    return k_ref[0, pl.ds(pl.multiple_of(j * TK, TK), TK), :]


def _fox_attn_kernel(q_ref, k_ref, vt_ref, fq_ref, fk_ref, o_ref, fcol_ref, s_ref, p_ref, acc_ref):
    i = pl.program_id(2)
    n_kb = fk_ref.shape[3] // TK

    @pl.when(i == 0)
    def _():
        for hh in range(HEADS_PER_STEP):
            for j in range(n_kb):
                rowv = fk_ref[0, 0, hh:hh + 1, j * TK:(j + 1) * TK]
                fcol_ref[hh, j * TK:(j + 1) * TK, :] = jnp.broadcast_to(rowv, (LANES, TK)).T

    q2 = q_ref[0]
    qhs = [_head_mask(q2, hh) for hh in range(HEADS_PER_STEP)]
    frows = [fq_ref[0, 0, hh:hh + 1, :] for hh in range(HEADS_PER_STEP)]
    rel = (lax.broadcasted_iota(jnp.int32, (TK, TQ), 0) - lax.broadcasted_iota(jnp.int32, (TK, TQ), 1))
    n_full = i * (TQ // TK)
    for hh in range(HEADS_PER_STEP):
        acc_ref[hh] = jnp.zeros((HEAD_DIM + ONES_ROWS, TQ), f32)
    ones = jnp.ones((ONES_ROWS, TK), bf16)

    def scores(j, slot, q0=0):
        kb = _key_block(k_ref, j)
        for hh in range(HEADS_PER_STEP):
            s_ref[slot, hh, :, q0:] = jnp.dot(kb, qhs[hh][:, q0:], preferred_element_type=f32)

    def softmax(j, slot, stats, masked, q0=0):
        out = []
        for hh in range(HEADS_PER_STEP):
            m, _ = stats[hh]
            fk = fcol_ref[hh, pl.ds(pl.multiple_of(j * TK, TK), TK), :]
            s = (s_ref[slot, hh, :, q0:] + frows[hh][:, q0:]) - jnp.concatenate([fk] * ((TQ - q0) // LANES), axis=1)
            if masked:
                s = jnp.where(rel[:, q0:] <= i * TQ - j * TK, s, NEG_INF)
            m_new = jnp.maximum(m[:, q0:], jnp.max(s, axis=0, keepdims=True))
            p_ref[slot, hh, :, q0:] = jnp.exp2((s - m_new).astype(bf16))
            alpha = jnp.exp2(m[:, q0:] - m_new)
            if q0:
                m_new = jnp.concatenate([m[:, :q0], m_new], axis=1)
                alpha = jnp.concatenate([jnp.ones((1, q0), f32), alpha], axis=1)
            out.append((m_new, alpha))
        return tuple(out)

    def pv(j, slot, stats, q0=0):
        for hh in range(HEADS_PER_STEP):
            vt = vt_ref[0, j, hh * HEAD_DIM:(hh + 1) * HEAD_DIM, :]
            upd = jnp.dot(jnp.concatenate([vt, ones], axis=0), p_ref[slot, hh, :, q0:], preferred_element_type=f32)
            acc_ref[hh, :, q0:] = stats[hh][1][:, q0:] * acc_ref[hh, :, q0:] + upd

    def finish():
        outs = []
        for hh in range(HEADS_PER_STEP):
            acc = acc_ref[hh]
            outs.append((acc[:HEAD_DIM] * (1.0 / acc[HEAD_DIM:HEAD_DIM + 1])).T)
        o_ref[0] = jnp.concatenate(outs, axis=1).astype(bf16)

    row = lambda v: jnp.full((1, TQ), v, f32)
    stats0 = tuple((row(NEG_INF), row(0.0)) for _ in range(HEADS_PER_STEP))
    d0, d1 = n_full, n_full + 1

    @pl.when(i == 0)
    def _():
        scores(d0, 0)
        scores(d1, 1, TK)
        st = softmax(d0, 0, stats0, True)
        pv(d0, 0, st)
        st = softmax(d1, 1, st, True, TK)
        pv(d1, 1, st, TK)
        finish()

    @pl.when(i > 0)
    def _():
        scores(d0, 0)
        scores(d1, 1, TK)
        st = softmax(d0, 0, stats0, True)
        pv(d0, 0, st)
        scores(0, 0)
        st = softmax(d1, 1, st, True, TK)

        pv(d1, 1, st, TK)
        scores(1, 1)
        st = softmax(0, 0, st, False)

        def two_ticks(u, st):
            j = 2 * u
            pv(j, 0, st)
            scores(j + 2, 0)
            st = softmax(j + 1, 1, st, False)
            pv(j + 1, 1, st)
            scores(j + 3, 1)
            return softmax(j + 2, 0, st, False)

        st = lax.fori_loop(0, i - 1, two_ticks, st)
        pv(n_full - 2, 0, st)
        st = softmax(n_full - 1, 1, st, False)
        pv(n_full - 1, 1, st)
        finish()


def _fox_attn(qt, k, vt, F):
    B, S, D = k.shape
    n_hp = N_HEADS // HEADS_PER_STEP
    F4 = F.reshape(B, n_hp, HEADS_PER_STEP, S)
    return pl.pallas_call(
        _fox_attn_kernel,
        grid=(B, n_hp, S // TQ),
        in_specs=[
            pl.BlockSpec((1, LANES, TQ), lambda b, h, i: (b, h, i)),
            pl.BlockSpec((1, S, LANES), lambda b, h, i: (b, 0, h)),
            pl.BlockSpec((1, S // TK, LANES, TK), lambda b, h, i: (b, 0, h, 0)),
            pl.BlockSpec((1, 1, HEADS_PER_STEP, TQ), lambda b, h, i: (b, h, 0, i)),
            pl.BlockSpec((1, 1, HEADS_PER_STEP, S), lambda b, h, i: (b, h, 0, 0)),
        ],
        out_specs=pl.BlockSpec((1, TQ, LANES), lambda b, h, i: (b, i, h)),
        out_shape=jax.ShapeDtypeStruct((B, S, D), bf16),
        scratch_shapes=[
            pltpu.VMEM((HEADS_PER_STEP, S, LANES), f32),
            pltpu.VMEM((2, HEADS_PER_STEP, TK, TQ), f32),
            pltpu.VMEM((2, HEADS_PER_STEP, TK, TQ), bf16),
            pltpu.VMEM((HEADS_PER_STEP, HEAD_DIM + ONES_ROWS, TQ), f32),
        ],
        compiler_params=_params("arbitrary", "arbitrary", "arbitrary"),
        name="fox_attention",
    )(qt, k, vt, F4, F4)


def _sb_attn_kernel(q_ref, k_ref, vt_ref, o_ref):
    i = pl.program_id(2)
    q2 = q_ref[0]
    qhs = [_head_mask(q2, hh) for hh in range(SB_HEADS)]
    rel = (lax.broadcasted_iota(jnp.int32, (TK, TQ), 0) - lax.broadcasted_iota(jnp.int32, (TK, TQ), 1))
    r = lax.broadcasted_iota(jnp.int32, (TK, TK), 0)
    c = lax.broadcasted_iota(jnp.int32, (TK, TK), 1)
    suffix = (r <= c).astype(bf16)

    def step(j, carry, diag, q0=0, q1=TQ):
        kb = _key_block(k_ref, j)
        out = []
        for hh in range(SB_HEADS):
            tail, acc = carry[hh]
            z = jnp.dot(kb, qhs[hh][:, q0:q1], preferred_element_type=f32)
            log_1m = -(jnp.maximum(z, 0.0) + jnp.log2(1.0 + jnp.exp2(-jnp.abs(z))))
            if diag:
                valid = rel[:, q0:q1] < i * TQ - j * TK
                log_1m = jnp.where(valid, log_1m, 0.0)
            incl = jnp.dot(suffix, log_1m.astype(bf16), preferred_element_type=f32)
            a = jnp.exp2((z + incl) + tail[:, q0:q1])
            if diag:
                a = jnp.where(valid, a, 0.0)
            vt = vt_ref[0, j, hh * HEAD_DIM:(hh + 1) * HEAD_DIM, :]
            upd = jnp.dot(vt, a.astype(bf16), preferred_element_type=f32)
            new = []
            for old, part in ((tail, tail[:, q0:q1] + incl[0:1, :]), (acc, acc[:, q0:q1] + upd)):
                pieces = ([old[:, :q0]] if q0 else []) + [part] + ([old[:, q1:]] if q1 < TQ else [])
                new.append(jnp.concatenate(pieces, axis=1) if len(pieces) > 1 else part)
            out.append(tuple(new))
        return tuple(out)

    def max_tail(carry):
        return functools.reduce(jnp.maximum, [jnp.max(tail) for tail, _ in carry])

    def finish(carry):
        o_ref[0] = jnp.concatenate([acc.T for _, acc in carry], axis=1).astype(bf16)

    init = tuple((jnp.zeros((1, TQ), f32), jnp.zeros((HEAD_DIM, TQ), f32)) for _ in range(SB_HEADS))
    n_full = i * (TQ // TK)

    def diagonal(carry):
        for d in reversed(range(TQ // TK)):
            carry = step(n_full + d, carry, diag=True, q0=d * TK)
        return carry

    @pl.when(i == 0)
    def _():
        finish(diagonal(init))

    @pl.when(i > 0)
    def _():
        carry = diagonal(init)
        late = functools.reduce(jnp.maximum, [jnp.max(tail[:, TK:]) for tail, _ in carry])
        carry = lax.cond(late < SB_UNDERFLOW,
                         lambda cr: step(n_full - 1, cr, diag=False, q1=TK),
                         lambda cr: step(n_full - 1, cr, diag=False), carry)

        def more(state):
            t, worst, _ = state
            return jnp.logical_and(t < n_full, worst >= SB_UNDERFLOW)

        def body(state):
            t, _, cr = state
            cr = step(n_full - 1 - t, cr, diag=False)
            return t + 1, max_tail(cr), cr

        _, _, carry = lax.while_loop(more, body, (jnp.int32(1), max_tail(carry), carry))
        finish(carry)


def _sb_attn(qt, k, vt):
    B, S, D = k.shape
    width = SB_HEADS * HEAD_DIM
    return pl.pallas_call(
        _sb_attn_kernel,
        grid=(B, N_HEADS // SB_HEADS, S // TQ),
        in_specs=[
            pl.BlockSpec((1, width, TQ), lambda b, h, i: (b, h, i)),
            pl.BlockSpec((1, S, width), lambda b, h, i: (b, 0, h)),
            pl.BlockSpec((1, S // TK, width, TK), lambda b, h, i: (b, 0, h, 0)),
        ],
        out_specs=pl.BlockSpec((1, TQ, width), lambda b, h, i: (b, i, h)),
        out_shape=jax.ShapeDtypeStruct((B, S, D), bf16),
        compiler_params=_params("arbitrary", "arbitrary", "arbitrary"),
        name="stickbreak_attention",
    )(qt, k, vt)


def _dwconv_kernel(xh_ref, x_ref, mod_ref, w_ref, b_ref, dw_ref, dwb_ref, lng_ref, lnb_ref, o_ref, ybuf_ref, z_ref):
    i = pl.program_id(1)
    mod = mod_ref[0, 0]
    xcat = jnp.concatenate([xh_ref[0], x_ref[0]], axis=0)
    h = (xcat * (1.0 + mod[1:2]) + mod[0:1]).astype(bf16)
    ag = jnp.dot(h, w_ref[...], preferred_element_type=f32) + b_ref[...]
    y = ag[:, :D_MODEL] * _sigmoid(ag[:, D_MODEL:])
    rowi = lax.broadcasted_iota(jnp.int32, y.shape, 0)
    ybuf_ref[0:CONV_HALO + TM, :] = jnp.where(jnp.logical_or(rowi >= CONV_HALO, i > 0), y, 0.0)
    ybuf_ref[CONV_HALO + TM:, :] = jnp.zeros((SUBLANES, D_MODEL), f32)
    first = CONV_HALO - (CONV_WIDTH - 1)
    for cb in range(D_MODEL // LANES):
        cols = slice(cb * LANES, (cb + 1) * LANES)
        acc = jnp.broadcast_to(dwb_ref[:, cols], (TM, LANES))
        for r in range(SUBLANES):
            z = None
            for a in range((CONV_HALO + SUBLANES) // SUBLANES):
                k = SUBLANES * a + r - first
                if 0 <= k < CONV_WIDTH:
                    term = ybuf_ref[SUBLANES * a:SUBLANES * a + TM + SUBLANES, cols] * dw_ref[k:k + 1, cols]
                    z = term if z is None else z + term
            if r == 0:
                acc = acc + z[:TM]
            else:
                z_ref[r] = z
                acc = acc + z_ref[r, r:r + TM, :]
        ybuf_ref[CONV_HALO:CONV_HALO + TM, cols] = acc
    yn = _ln(ybuf_ref[CONV_HALO:CONV_HALO + TM, :], lng_ref[...], lnb_ref[...])
    o_ref[0] = (yn * _sigmoid(yn)).astype(bf16)


def _conv_pre(x, mod_l, w_in, b_in, dw, dw_b, ln_g, ln_b):
    B, S, D = x.shape
    per = TM // CONV_HALO
    row = lambda b, i: (b, i, 0)
    return pl.pallas_call(
        _dwconv_kernel,
        grid=(B, S // TM),
        in_specs=[
            pl.BlockSpec((1, CONV_HALO, D), lambda b, i: (b, jnp.maximum(i * per - 1, 0), 0)),
            pl.BlockSpec((1, TM, D), row),
            pl.BlockSpec((1, 1, 6, D), lambda b, i: (b, 0, 0, 0)),
            _const_spec((D, 2 * D)),
            _const_spec((1, 2 * D)),
            _const_spec((CONV_WIDTH, D)),
            _const_spec((1, D)),
            _const_spec((1, D)),
            _const_spec((1, D)),
        ],
        out_specs=pl.BlockSpec((1, TM, D), row),
        out_shape=jax.ShapeDtypeStruct((B, S, D), bf16),
        scratch_shapes=[
            pltpu.VMEM((CONV_HALO + TM + SUBLANES, D), f32),
            pltpu.VMEM((SUBLANES, TM + SUBLANES, LANES), f32),
        ],
        compiler_params=_params("arbitrary", "arbitrary"),
        name="conv_glu_dw_ln_swish",
    )(x, x, mod_l.reshape(B, 1, 6, D), w_in, b_in, dw, dw_b, ln_g, ln_b)


def kernel(x, c, mod_w, mod_b, ln1_g, ln1_b, ln2_g, ln2_b, ffn_w_in, ffn_w_out, gm_w_in, gm_b_in, gm_ln_g, gm_ln_b, gm_w_s, gm_b_s, gm_w_out, fox_w_in, fox_b_f, fox_w_out, sb_w_in, sb_w_out, cv_w_in, cv_b_in, cv_dw, cv_dw_b, cv_ln_g, cv_ln_b, cv_w_out, cv_b_out):
    B, S, D = x.shape
    assert (B, S, D) == (16, 4096, D_MODEL) and S % TM_POST == 0 and TM_POST % TM == 0
    assert TQ == 2 * TK and TM == TK
    row = lambda v: v.reshape(1, -1)
    mod = _modulation(c, mod_w, mod_b)
    zero_bias = jnp.zeros((1, D), f32)
    for l in range(DEPTH):
        m, j = l % 4, l // 4
        if m == 0:
            bs_full = jnp.repeat(gm_b_s[j].T, GM_GROUP_DIM, axis=1)
            ypre = _gmlp_pre(x, mod[l], gm_w_in[j].astype(bf16), row(gm_b_in[j]), row(gm_ln_g[j]),
                             row(gm_ln_b[j]), gm_w_s[j], bs_full)
            w_mo, b_mo = gm_w_out[j], zero_bias
        elif m == 1:
            w = fox_w_in[j]
            w_f = jnp.pad(w[:, 3 * D:], ((0, 0), (0, LANES - N_HEADS))).astype(bf16)
            q, k, vt, lf = _attn_in(x, mod[l], w[:, :3 * D].astype(bf16), w_f, fox_b_f[j].reshape(N_HEADS, 1))
            ypre = _fox_attn(q, k, vt, _cumsum(lf))
            w_mo, b_mo = fox_w_out[j], zero_bias
        elif m == 2:
            q, k, vt = _attn_in(x, mod[l], sb_w_in[j].astype(bf16))
            ypre = _sb_attn(q, k, vt)
            w_mo, b_mo = sb_w_out[j], zero_bias
        else:
            ypre = _conv_pre(x, mod[l], cv_w_in[j].astype(bf16), row(cv_b_in[j]), cv_dw[j], row(cv_dw_b[j]),
                             row(cv_ln_g[j]), row(cv_ln_b[j]))
            w_mo, b_mo = cv_w_out[j], row(cv_b_out[j])
        x = _post(x, ypre, mod[l], w_mo.astype(bf16), b_mo, row(ln1_g[l]), row(ln1_b[l]),
                  ffn_w_in[l].astype(bf16), ffn_w_out[l].astype(bf16), row(ln2_g[l]), row(ln2_b[l]))
    return x
```
